```python
import jax, jax.numpy as jnp
from jax import lax
import numpy as np

D_MODEL = 2048
BATCH = 4
SEQ = 2048
DEPTH = 4
DEC_BATCH = 16
DEC_SEQ = 64
PAST_LEN = 4096

CHUNK = 64

CONV_DIM = D_MODEL // 4
CONV_HEADS = 4
CONV_WIDTH = 31
POOL_DIM = D_MODEL // 4
POOL_WINDOWS = (2, 4, 8, 16)
POOL_GROUPS = len(POOL_WINDOWS)
POOL_GROUP_DIM = POOL_DIM // POOL_GROUPS
POOL_MAX = max(POOL_WINDOWS)
RWKV_DIM = D_MODEL // 2
HEAD_SIZE = 64
RWKV_HEADS = RWKV_DIM // HEAD_SIZE
DECAY_LORA = 64
AAA_LORA = 64
GATE_LORA = 64
RWKV_PROJ = 3 * RWKV_DIM + DECAY_LORA + AAA_LORA + GATE_LORA
MIX_DIM = CONV_DIM + POOL_DIM + RWKV_DIM
IN_PROJ = 2 * CONV_DIM + POOL_DIM + RWKV_PROJ
D_FF = -(-(8 * D_MODEL) // (3 * 256)) * 256

RMS_EPS = 1e-6
LN_EPS = 1e-5
GN_EPS = 64e-5

kernel_name = 'hybrid_conv_pool_rwkv7_stream_step'


def rms_norm(x, g):
    xf = x.astype(jnp.float32)
    y = xf * lax.rsqrt(jnp.mean(xf * xf, axis=-1, keepdims=True) + RMS_EPS)
    return (y * g.astype(jnp.float32)).astype(x.dtype)


def conv_mixer(z, hist, conv_w, conv_b, ln_g, ln_b):
    val, gate = jnp.split(z, 2, axis=-1)
    u = val * jax.nn.sigmoid(gate)
    u_pad = jnp.concatenate([hist.astype(u.dtype), u], axis=1)
    h = lax.conv_general_dilated(u_pad, conv_w[:, None, :].astype(u.dtype), window_strides=(1,),
                                 padding='VALID', dimension_numbers=('NWC', 'WIO', 'NWC'),
                                 feature_group_count=CONV_DIM) + conv_b
    hf = h.astype(jnp.float32)
    mu = jnp.mean(hf, axis=-1, keepdims=True)
    var = jnp.mean(jnp.square(hf - mu), axis=-1, keepdims=True)
    hn = (hf - mu) * lax.rsqrt(var + LN_EPS) * ln_g.astype(jnp.float32) + ln_b.astype(jnp.float32)
    out = jax.nn.silu(hn).astype(z.dtype)
    return out, u_pad[:, -(CONV_WIDTH - 1):]


def pool_mixer(p, hist, start_pos, pool_w, pool_scale):
    B, L, _ = p.shape
    p_pad = jnp.concatenate([hist.astype(p.dtype), p], axis=1)
    pf = p_pad.astype(jnp.float32)
    csum = jnp.concatenate([jnp.zeros_like(pf[:, :1]), jnp.cumsum(pf, axis=1)], axis=1)
    end = csum[:, POOL_MAX:]
    pos = start_pos + jnp.arange(L)
    means = []
    for gi, w in enumerate(POOL_WINDOWS):
        sl = slice(gi * POOL_GROUP_DIM, (gi + 1) * POOL_GROUP_DIM)
        begin = csum[:, POOL_MAX - w:POOL_MAX - w + L, sl]
        cnt = jnp.minimum(w, pos + 1).astype(jnp.float32)[None, :, None]
        means.append((end[..., sl] - begin) / cnt)
    d = (jnp.concatenate(means, axis=-1) - pf[:, POOL_MAX - 1:]).astype(p.dtype)
    d = jnp.einsum('blgc,gcd->blgd', d.reshape(B, L, POOL_GROUPS, POOL_GROUP_DIM), pool_w)
    out = d.reshape(B, L, POOL_DIM) * pool_scale
    return out, p_pad[:, -(POOL_MAX - 1):]


def rwkv_mixer(q, shift_prev, wkv_state, mu, w0, w_up, a0, a_up, g_up, k_k, k_a, r_k, gn_g, gn_b):
    B, L, _ = q.shape
    H, N = RWKV_HEADS, HEAD_SIZE
    f32 = jnp.float32
    q_prev = jnp.concatenate([shift_prev.astype(q.dtype), q[:, :-1]], axis=1)
    qs = q + (q_prev - q) * mu
    c0 = 3 * RWKV_DIM
    r, k, v, w_lo, a_lo, g_lo = jnp.split(
        qs, [RWKV_DIM, 2 * RWKV_DIM, c0, c0 + DECAY_LORA, c0 + DECAY_LORA + AAA_LORA], axis=-1)
    w = -jax.nn.softplus(-(w0 + jnp.tanh(w_lo) @ w_up).astype(f32)) - 0.5
    decay = jnp.exp(-jnp.exp(w))
    a = jax.nn.sigmoid((a0 + a_lo @ a_up).astype(f32))
    g = jax.nn.sigmoid(g_lo) @ g_up
    hd = lambda t: t.astype(f32).reshape(B, L, H, N)
    r, k, v, decay, a = hd(r), hd(k), hd(v), hd(decay), hd(a)
    kk = k * k_k.astype(f32).reshape(H, N)
    kk = kk * lax.rsqrt(jnp.maximum(jnp.sum(kk * kk, axis=-1, keepdims=True), 1e-24))
    k = k * (1.0 + (a - 1.0) * k_a.astype(f32).reshape(H, N))

    def step(S, inp):
        r_t, w_t, k_t, v_t, kk_t, a_t = inp
        sa = jnp.einsum('bhij,bhj->bhi', S, -kk_t)
        S = (S * w_t[:, :, None, :] + sa[..., None] * (kk_t * a_t)[:, :, None, :]
             + v_t[..., None] * k_t[:, :, None, :])
        return S, jnp.einsum('bhij,bhj->bhi', S, r_t)

    xs = tuple(jnp.moveaxis(t, 1, 0) for t in (r, decay, k, v, kk, a))
    S_final, y = lax.scan(step, wkv_state.astype(f32), xs)
    y = jnp.moveaxis(y, 0, 1)
    ym = jnp.mean(y, axis=-1, keepdims=True)
    yv = jnp.mean(jnp.square(y - ym), axis=-1, keepdims=True)
    yn = (y - ym) * lax.rsqrt(yv + GN_EPS) * gn_g.astype(f32).reshape(H, N) + gn_b.astype(f32).reshape(H, N)
    bonus = jnp.sum(r * k * r_k.astype(f32), axis=-1, keepdims=True) * v
    out = (yn + bonus).reshape(B, L, RWKV_DIM).astype(q.dtype) * g
    return out, q[:, -1:], S_final


def trunk(x, cache_conv, cache_pool, state_shift, state_wkv, start_pos,
          norm_mix, w_in, conv_w, conv_b, conv_ln_g, conv_ln_b, pool_w, pool_scale,
          shift_mu, decay_w0, decay_up, iclr_a0, iclr_up, gate_up, k_k, k_a, r_k, gn_g, gn_b,
          w_out, norm_ffn, ffn_gate, ffn_up, ffn_down, norm_final):
    conv_list, pool_list, shift_list, wkv_list = [], [], [], []
    for l in range(DEPTH):
        h = rms_norm(x, norm_mix[l])
        z = h @ w_in[l]
        z_conv, z_pool, z_rwkv = jnp.split(z, [2 * CONV_DIM, 2 * CONV_DIM + POOL_DIM], axis=-1)
        o_conv, c_new = conv_mixer(z_conv, cache_conv[l], conv_w[l], conv_b[l], conv_ln_g[l], conv_ln_b[l])
        o_pool, p_new = pool_mixer(z_pool, cache_pool[l], start_pos, pool_w[l], pool_scale[l])
        o_rwkv, s_new, S_new = rwkv_mixer(z_rwkv, state_shift[l], state_wkv[l], shift_mu[l], decay_w0[l],
                                          decay_up[l], iclr_a0[l], iclr_up[l], gate_up[l], k_k[l], k_a[l],
                                          r_k[l], gn_g[l], gn_b[l])
        x = x + jnp.concatenate([o_conv, o_pool, o_rwkv], axis=-1) @ w_out[l]
        h = rms_norm(x, norm_ffn[l])
        x = x + (jax.nn.silu(h @ ffn_gate[l]) * (h @ ffn_up[l])) @ ffn_down[l]
        conv_list.append(c_new)
        pool_list.append(p_new)
        shift_list.append(s_new)
        wkv_list.append(S_new)
    y = rms_norm(x, norm_final)
    return (y, jnp.stack(conv_list), jnp.stack(pool_list), jnp.stack(shift_list), jnp.stack(wkv_list))


def setup_inputs(seed: int = 0) -> dict:
    key = jax.random.key(seed)
    ks = iter(jax.random.split(key, 40))
    nrm = lambda shape, s: jax.random.normal(next(ks), shape, jnp.float32) * s
    uni = lambda shape, lo, hi: jax.random.uniform(next(ks), shape, jnp.float32, lo, hi)
    H, N = RWKV_HEADS, HEAD_SIZE
    return {
        'x_prompt': nrm((BATCH, SEQ, D_MODEL), 1.0),
        'x_sample': nrm((DEC_BATCH, DEC_SEQ, D_MODEL), 1.0),
        'cache_conv': nrm((DEPTH, DEC_BATCH, CONV_WIDTH - 1, CONV_DIM), 0.5),
        'cache_pool': nrm((DEPTH, DEC_BATCH, POOL_MAX - 1, POOL_DIM), 1.0),
        'state_shift': nrm((DEPTH, DEC_BATCH, 1, RWKV_PROJ), 1.0),
        'state_wkv': nrm((DEPTH, DEC_BATCH, H, N, N), 0.5),
        'norm_mix': 1.0 + nrm((DEPTH, D_MODEL), 0.05),
        'w_in': nrm((DEPTH, D_MODEL, IN_PROJ), D_MODEL ** -0.5),
        'conv_w': nrm((DEPTH, CONV_WIDTH, CONV_DIM), CONV_WIDTH ** -0.5),
        'conv_b': nrm((DEPTH, CONV_DIM), 0.02),
        'conv_ln_g': 1.0 + nrm((DEPTH, CONV_DIM), 0.05),
        'conv_ln_b': nrm((DEPTH, CONV_DIM), 0.02),
        'pool_w': nrm((DEPTH, POOL_GROUPS, POOL_GROUP_DIM, POOL_GROUP_DIM), POOL_GROUP_DIM ** -0.5),
        'pool_scale': uni((DEPTH, POOL_DIM), 0.5, 1.5),
        'shift_mu': uni((DEPTH, RWKV_PROJ), 0.0, 1.0),
        'decay_w0': uni((DEPTH, RWKV_DIM), -5.0, 1.0),
        'decay_up': nrm((DEPTH, DECAY_LORA, RWKV_DIM), 0.1),
        'iclr_a0': nrm((DEPTH, RWKV_DIM), 0.1),
        'iclr_up': nrm((DEPTH, AAA_LORA, RWKV_DIM), AAA_LORA ** -0.5),
        'gate_up': nrm((DEPTH, GATE_LORA, RWKV_DIM), GATE_LORA ** -0.5),
        'k_k': 0.85 + nrm((DEPTH, RWKV_DIM), 0.05),
        'k_a': 1.0 + nrm((DEPTH, RWKV_DIM), 0.05),
        'r_k': nrm((DEPTH, H, N), 0.1),
        'gn_g': 1.0 + nrm((DEPTH, RWKV_DIM), 0.05),
        'gn_b': nrm((DEPTH, RWKV_DIM), 0.02),
        'w_out': nrm((DEPTH, MIX_DIM, D_MODEL), MIX_DIM ** -0.5),
        'norm_ffn': 1.0 + nrm((DEPTH, D_MODEL), 0.05),
        'ffn_gate': nrm((DEPTH, D_MODEL, D_FF), D_MODEL ** -0.5),
        'ffn_up': nrm((DEPTH, D_MODEL, D_FF), D_MODEL ** -0.5),
        'ffn_down': nrm((DEPTH, D_FF, D_MODEL), D_FF ** -0.5),
        'norm_final': 1.0 + nrm((D_MODEL,), 0.05),
    }


def reference(x_prompt, x_sample, cache_conv, cache_pool, state_shift, state_wkv,
              norm_mix, w_in, conv_w, conv_b, conv_ln_g, conv_ln_b, pool_w, pool_scale,
              shift_mu, decay_w0, decay_up, iclr_a0, iclr_up, gate_up, k_k, k_a, r_k, gn_g, gn_b,
              w_out, norm_ffn, ffn_gate, ffn_up, ffn_down, norm_final):
    weights = (norm_mix, w_in, conv_w, conv_b, conv_ln_g, conv_ln_b, pool_w, pool_scale,
               shift_mu, decay_w0, decay_up, iclr_a0, iclr_up, gate_up, k_k, k_a, r_k, gn_g, gn_b,
               w_out, norm_ffn, ffn_gate, ffn_up, ffn_down, norm_final)
    bp = x_prompt.shape[0]
    dt = x_prompt.dtype
    zc = jnp.zeros((DEPTH, bp, CONV_WIDTH - 1, CONV_DIM), dt)
    zp = jnp.zeros((DEPTH, bp, POOL_MAX - 1, POOL_DIM), dt)
    zs = jnp.zeros((DEPTH, bp, 1, RWKV_PROJ), dt)
    zw = jnp.zeros((DEPTH, bp, RWKV_HEADS, HEAD_SIZE, HEAD_SIZE), jnp.float32)
    y_prompt, p_conv, p_pool, p_shift, p_wkv = trunk(x_prompt, zc, zp, zs, zw, 0, *weights)
    y_sample, s_conv, s_pool, s_shift, s_wkv = trunk(x_sample, cache_conv, cache_pool, state_shift,
                                                     state_wkv, PAST_LEN, *weights)
    return (y_prompt, y_sample, p_conv, p_pool, p_shift, p_wkv, s_conv, s_pool, s_shift, s_wkv)
```

```python
import functools

import jax
import jax.numpy as jnp
from jax import lax
from jax.experimental import pallas as pl
from jax.experimental.pallas import tpu as pltpu

F32 = jnp.float32
BF16 = jnp.bfloat16

D_MODEL = 2048
DEPTH = 4
PAST_LEN = 4096
CONV_DIM = 512
CONV_WIDTH = 31
CONV_HIST = CONV_WIDTH - 1
POOL_DIM = 512
POOL_WINDOWS = (2, 4, 8, 16)
POOL_GROUP_DIM = 128
POOL_HIST = max(POOL_WINDOWS) - 1
RWKV_DIM = 1024
HEAD_SIZE = 64
RWKV_HEADS = RWKV_DIM // HEAD_SIZE
LORA = 64
RWKV_PROJ = 3 * RWKV_DIM + 3 * LORA
IN_PROJ = 2 * CONV_DIM + POOL_DIM + RWKV_PROJ
POOL_OFF = 2 * CONV_DIM
RWKV_OFF = POOL_OFF + POOL_DIM
D_FF = 5632
RMS_EPS = 1e-6
LN_EPS = 1e-5
GN_EPS = 64e-5

LANES = 128
PAIRS = RWKV_DIM // LANES
CHUNK = 64
SOLVE_SQUARINGS = 5
VMEM_LIMIT = 56 * 1024 * 1024


def _tiles(n_tokens):
    tm = min(n_tokens, 1024)
    assert n_tokens % tm == 0
    return dict(tm=tm, tn=512, tk_ff=1408)


def _bdot(a, b):
    return jnp.dot(a.astype(BF16), b.astype(BF16), preferred_element_type=F32)


def _bdot_nt(a, b):
    return lax.dot_general(a.astype(BF16), b.astype(BF16), (((1,), (1,)), ((), ())),
                           preferred_element_type=F32)


def _sigmoid(x):
    return 1.0 / (1.0 + jnp.exp(-x))


def _norm_in_kernel(x_ref, g_ref, w_ref, o_ref, h_ref):
    @pl.when(pl.program_id(1) == 0)
    def _():
        x = x_ref[...]
        ms = jnp.mean(x * x, axis=-1, keepdims=True)
        h_ref[...] = (x * lax.rsqrt(ms + RMS_EPS) * g_ref[...]).astype(BF16)

    o_ref[...] = jnp.dot(h_ref[...], w_ref[...].astype(BF16), preferred_element_type=F32)


def _norm_in_proj(x, g, w, layer):
    T = x.shape[0]
    t = _tiles(T)
    tm, tn = t["tm"], t["tn"]
    N = w.shape[-1]
    return pl.pallas_call(
        _norm_in_kernel,
        grid=(T // tm, pl.cdiv(N, tn)),
        in_specs=[
            pl.BlockSpec((tm, D_MODEL), lambda m, n: (m, 0)),
            pl.BlockSpec((None, 1, D_MODEL), lambda m, n: (layer, 0, 0)),
            pl.BlockSpec((None, D_MODEL, tn), lambda m, n: (layer, 0, n)),
        ],
        out_specs=pl.BlockSpec((tm, tn), lambda m, n: (m, n)),
        out_shape=jax.ShapeDtypeStruct((T, N), F32),
        scratch_shapes=[pltpu.VMEM((tm, D_MODEL), BF16)],
        compiler_params=pltpu.CompilerParams(
            dimension_semantics=("arbitrary", "arbitrary"), vmem_limit_bytes=VMEM_LIMIT),
    )(x, g, w)


def _out_kernel(x_ref, a_ref, w_ref, o_ref):
    o_ref[...] = x_ref[...] + jnp.dot(a_ref[...], w_ref[...].astype(BF16),
                                      preferred_element_type=F32)


def _out_proj(x, a, w, layer):
    T = x.shape[0]
    t = _tiles(T)
    tm, tn = t["tm"], t["tn"]
    K = a.shape[-1]
    return pl.pallas_call(
        _out_kernel,
        grid=(T // tm, D_MODEL // tn),
        in_specs=[
            pl.BlockSpec((tm, tn), lambda m, n: (m, n)),
            pl.BlockSpec((tm, K), lambda m, n: (m, 0)),
            pl.BlockSpec((None, K, tn), lambda m, n: (layer, 0, n)),
        ],
        out_specs=pl.BlockSpec((tm, tn), lambda m, n: (m, n)),
        out_shape=jax.ShapeDtypeStruct((T, D_MODEL), F32),
        compiler_params=pltpu.CompilerParams(
            dimension_semantics=("arbitrary", "arbitrary"), vmem_limit_bytes=VMEM_LIMIT),
    )(x, a, w)


def _ffn_up_kernel(x_ref, g_ref, wg_ref, wu_ref, o_ref, h_ref):
    @pl.when(pl.program_id(1) == 0)
    def _():
        x = x_ref[...]
        ms = jnp.mean(x * x, axis=-1, keepdims=True)
        h_ref[...] = (x * lax.rsqrt(ms + RMS_EPS) * g_ref[...]).astype(BF16)

    h = h_ref[...]
    gate = jnp.dot(h, wg_ref[...].astype(BF16), preferred_element_type=F32)
    up = jnp.dot(h, wu_ref[...].astype(BF16), preferred_element_type=F32)
    o_ref[...] = (gate * _sigmoid(gate) * up).astype(BF16)


def _ffn_up(x, g, wg, wu, layer):
    T = x.shape[0]
    t = _tiles(T)
    tm, tn = t["tm"], t["tn"]
    return pl.pallas_call(
        _ffn_up_kernel,
        grid=(T // tm, D_FF // tn),
        in_specs=[
            pl.BlockSpec((tm, D_MODEL), lambda m, n: (m, 0)),
            pl.BlockSpec((None, 1, D_MODEL), lambda m, n: (layer, 0, 0)),
            pl.BlockSpec((None, D_MODEL, tn), lambda m, n: (layer, 0, n)),
            pl.BlockSpec((None, D_MODEL, tn), lambda m, n: (layer, 0, n)),
        ],
        out_specs=pl.BlockSpec((tm, tn), lambda m, n: (m, n)),
        out_shape=jax.ShapeDtypeStruct((T, D_FF), BF16),
        scratch_shapes=[pltpu.VMEM((tm, D_MODEL), BF16)],
        compiler_params=pltpu.CompilerParams(
            dimension_semantics=("arbitrary", "arbitrary"), vmem_limit_bytes=VMEM_LIMIT),
    )(x, g, wg, wu)


def _ffn_down_kernel(x_ref, a_ref, w_ref, o_ref, acc_ref):
    k = pl.program_id(2)

    @pl.when(k == 0)
    def _():
        acc_ref[...] = x_ref[...]

    acc_ref[...] += jnp.dot(a_ref[...], w_ref[...].astype(BF16), preferred_element_type=F32)

    @pl.when(k == pl.num_programs(2) - 1)
    def _():
        o_ref[...] = acc_ref[...]


def _ffn_down(x, a, w, layer):
    T = x.shape[0]
    t = _tiles(T)
    tm, tn, tk = t["tm"], t["tn"], t["tk_ff"]
    return pl.pallas_call(
        _ffn_down_kernel,
        grid=(T // tm, D_MODEL // tn, D_FF // tk),
        in_specs=[
            pl.BlockSpec((tm, tn), lambda m, n, k: (m, n)),
            pl.BlockSpec((tm, tk), lambda m, n, k: (m, k)),
            pl.BlockSpec((None, tk, tn), lambda m, n, k: (layer, k, n)),
        ],
        out_specs=pl.BlockSpec((tm, tn), lambda m, n, k: (m, n)),
        out_shape=jax.ShapeDtypeStruct((T, D_MODEL), F32),
        scratch_shapes=[pltpu.VMEM((tm, tn), F32)],
        compiler_params=pltpu.CompilerParams(
            dimension_semantics=("arbitrary", "arbitrary", "arbitrary"),
            vmem_limit_bytes=VMEM_LIMIT),
    )(x, a, w)


def _final_norm_kernel(x_ref, g_ref, o_ref):
    x = x_ref[...]
    ms = jnp.mean(x * x, axis=-1, keepdims=True)
    o_ref[...] = x * lax.rsqrt(ms + RMS_EPS) * g_ref[...]


def _final_norm(x, g):
    T = x.shape[0]
    tm = min(T, 512)
    return pl.pallas_call(
        _final_norm_kernel,
        grid=(T // tm,),
        in_specs=[
            pl.BlockSpec((tm, D_MODEL), lambda m: (m, 0)),
            pl.BlockSpec((1, D_MODEL), lambda m: (0, 0)),
        ],
        out_specs=pl.BlockSpec((tm, D_MODEL), lambda m: (m, 0)),
        out_shape=jax.ShapeDtypeStruct((T, D_MODEL), F32),
        compiler_params=pltpu.CompilerParams(dimension_semantics=("arbitrary",)),
    )(x, g)


def _conv_group(z_ref, ubuf, conv_w_ref, conv_b_ref, ln_g_ref, ln_b_ref):
    C = CHUNK
    val = z_ref[:, 0:CONV_DIM]
    gate = z_ref[:, CONV_DIM:2 * CONV_DIM]
    ubuf[32:32 + C, :] = val * _sigmoid(gate)
    acc = jnp.zeros((C, CONV_DIM), F32) + conv_b_ref[...]
    for j in range(CONV_WIDTH):
        acc = acc + ubuf[2 + j:2 + j + C, :] * conv_w_ref[j:j + 1, :]
    mu = jnp.mean(acc, axis=-1, keepdims=True)
    cen = acc - mu
    var = jnp.mean(cen * cen, axis=-1, keepdims=True)
    hn = cen * lax.rsqrt(var + LN_EPS) * ln_g_ref[...] + ln_b_ref[...]
    return hn * _sigmoid(hn)


def _pool_group(z_ref, pbuf, pool_w_ref, pool_scale_ref, pos0):
    C = CHUNK
    pbuf[16:16 + C, :] = z_ref[:, POOL_OFF:POOL_OFF + POOL_DIM]
    pos = pos0 + lax.broadcasted_iota(jnp.int32, (C, POOL_GROUP_DIM), 0)
    outs = []
    for gi, w in enumerate(POOL_WINDOWS):
        lo, hi = gi * POOL_GROUP_DIM, (gi + 1) * POOL_GROUP_DIM
        tok = pbuf[16:16 + C, lo:hi]
        tot = tok
        for j in range(1, w):
            tot = tot + pbuf[16 - j:16 - j + C, lo:hi]
        cnt = jnp.minimum(w, pos + 1).astype(F32)
        d = tot / cnt - tok
        outs.append(_bdot(d, pool_w_ref[gi]))
    return jnp.concatenate(outs, axis=-1) * pool_scale_ref[...]


def _head_sums(x, ones_bd):
    stacked = jnp.concatenate([x[:, p * LANES:(p + 1) * LANES] for p in range(PAIRS)], axis=0)
    s = _bdot(stacked, ones_bd)
    C = x.shape[0]
    return jnp.concatenate([s[p * C:(p + 1) * C, :] for p in range(PAIRS)], axis=-1)


def _wkv_chunk(r, lw, k2, v, kk, a, s_ref, ones_bd):
    C = CHUNK
    row = lax.broadcasted_iota(jnp.int32, (C, C), 0)
    col = lax.broadcasted_iota(jnp.int32, (C, C), 1)
    tri = (col <= row).astype(F32)
    cum = jnp.dot(tri, lw, preferred_element_type=F32, precision=lax.Precision.HIGHEST)
    tot = cum[C - 1:C, :]
    g_in = jnp.exp(cum)
    g_ex = jnp.exp(cum - lw)
    g_inv = jnp.exp(-cum)
    g_end = jnp.exp(tot - cum)
    b = kk * a
    A_t = -kk * g_ex
    R_t = r * g_in
    B_t = b * g_inv
    K_t = k2 * g_inv
    B_e = b * g_end
    K_e = k2 * g_end
    g_tot = jnp.exp(tot)

    lane = lax.broadcasted_iota(jnp.int32, (2 * C, LANES), 1)
    low = lane < HEAD_SIZE
    low_c = lax.broadcasted_iota(jnp.int32, (C, LANES), 1) < HEAD_SIZE
    r2 =lax.broadcasted_iota(jnp.int32, (C, 2 * C), 0)
    c2 = lax.broadcasted_iota(jnp.int32, (C, 2 * C), 1)
    c2 = jnp.where(c2 >= C, c2 - C, c2)
    strict = c2 < r2
    incl = c2 <= r2
    rr = lax.broadcasted_iota(jnp.int32, (LANES, LANES), 0)
    cc = lax.broadcasted_iota(jnp.int32, (LANES, LANES), 1)
    same_head = (rr < HEAD_SIZE) == (cc < HEAD_SIZE)
    zeros_half = jnp.zeros((C, 2 * C), F32)

    ys = []
    for p in range(PAIRS):
        sl = slice(p * LANES, (p + 1) * LANES)
        AR = jnp.concatenate([A_t[:, sl], R_t[:, sl]], axis=0)
        BK = jnp.concatenate([B_t[:, sl], K_t[:, sl]], axis=0)
        vp = v[:, sl]
        S0 = s_ref[p]
        AS = _bdot_nt(AR, S0)
        rhs = jnp.concatenate([AS[:C], vp], axis=0)
        y_heads = []
        u_heads = []
        for first in (True, False):
            keep = low if first else jnp.logical_not(low)
            G = _bdot_nt(jnp.where(keep, AR, 0.0), BK)
            P = jnp.where(strict, G[:C], 0.0)
            M = jnp.where(incl, G[C:], 0.0)
            U = rhs[:C] + _bdot(P, rhs)
            for _ in range(SOLVE_SQUARINGS):
                P = _bdot(P, jnp.concatenate([P, zeros_half], axis=0))
                U = U + _bdot(P, jnp.concatenate([U, vp], axis=0))
            u_heads.append(U)
            y_heads.append(_bdot(M, jnp.concatenate([U, vp], axis=0)))
        Up = jnp.where(low_c, u_heads[0], u_heads[1])
        ys.append(AS[C:] + jnp.where(low_c, y_heads[0], y_heads[1]))
        UV = jnp.concatenate([Up, vp], axis=0)
        BKe = jnp.concatenate([B_e[:, sl], K_e[:, sl]], axis=0)
        upd = _bdot(UV.T, BKe)
        s_ref[p] = S0 * g_tot[:, sl] + jnp.where(same_head, upd, 0.0)
    return jnp.concatenate(ys, axis=-1)


def _rwkv_group(z_ref, qprev, s_ref, ones_bd, mu_ref, w0_ref, w_up_ref, a0_ref, a_up_ref,
                g_up_ref, k_k_ref, k_a_ref, r_k_ref, gn_g_ref, gn_b_ref):
    C = CHUNK
    q = z_ref[:, RWKV_OFF:RWKV_OFF + RWKV_PROJ]
    rolled = pltpu.roll(q, 1, 0)
    first_row = lax.broadcasted_iota(jnp.int32, (C, RWKV_PROJ), 0) == 0
    q_prev = jnp.where(first_row, qprev[0:1, :], rolled)
    qs = q + (q_prev - q) * mu_ref[...]
    qprev[0:1, :] = q[C - 1:C, :]

    c0 = 3 * RWKV_DIM
    r = qs[:, 0:RWKV_DIM]
    k = qs[:, RWKV_DIM:2 * RWKV_DIM]
    v = qs[:, 2 * RWKV_DIM:c0]
    w_lo = qs[:, c0:c0 + LORA]
    a_lo = qs[:, c0 + LORA:c0 + 2 * LORA]
    g_lo = qs[:, c0 + 2 * LORA:c0 + 3 * LORA]

    wraw = w0_ref[...] + _bdot(jnp.tanh(w_lo), w_up_ref[...])
    neg = -wraw
    softplus = jnp.maximum(neg, 0.0) + jnp.log(1.0 + jnp.exp(-jnp.abs(neg)))
    lw = -jnp.exp(-softplus - 0.5)
    a = _sigmoid(a0_ref[...] + _bdot(a_lo, a_up_ref[...]))
    g = _bdot(_sigmoid(g_lo), g_up_ref[...])

    kk = k * k_k_ref[...]
    kk = kk * lax.rsqrt(jnp.maximum(_head_sums(kk * kk, ones_bd), 1e-24))
    k2 = k * (1.0 + (a - 1.0) * k_a_ref[...])

    y = _wkv_chunk(r, lw, k2, v, kk, a, s_ref, ones_bd)

    inv_n = 1.0 / HEAD_SIZE
    ym = _head_sums(y, ones_bd) * inv_n
    yc = y - ym
    yv = _head_sums(yc * yc, ones_bd) * inv_n
    yn = yc * lax.rsqrt(yv + GN_EPS) * gn_g_ref[...] + gn_b_ref[...]
    bonus = _head_sums(r * k2 * r_k_ref[...], ones_bd) * v
    return (yn + bonus) * g


def _mixer_kernel(start_pos, z_ref, hc_ref, hp_ref, hs_ref, s0_ref, ones_ref,
                  conv_w_ref, conv_b_ref, ln_g_ref, ln_b_ref, pool_w_ref, pool_scale_ref,
                  mu_ref, w0_ref, w_up_ref, a0_ref, a_up_ref, g_up_ref, k_k_ref, k_a_ref,
                  r_k_ref, gn_g_ref, gn_b_ref,
                  mix_ref, newc_ref, newp_ref, news_ref, news_wkv_ref,
                  ubuf, pbuf, qprev, s_scr):
    C = CHUNK
    t = pl.program_id(1)

    @pl.when(t == 0)
    def _():
        ubuf[0:2, :] = jnp.zeros((2, CONV_DIM), F32)
        ubuf[2:32, :] = hc_ref[...]
        pbuf[0:1, :] = jnp.zeros((1, POOL_DIM), F32)
        pbuf[1:16, :] = hp_ref[...]
        qprev[...] = jnp.zeros(qprev.shape, F32)
        qprev[0:1, :] = hs_ref[...]
        zero = jnp.zeros((HEAD_SIZE, HEAD_SIZE), F32)
        for p in range(PAIRS):
            top = jnp.concatenate([s0_ref[2 * p], zero], axis=-1)
            bot = jnp.concatenate([zero, s0_ref[2 * p + 1]], axis=-1)
            s_scr[p] = jnp.concatenate([top, bot], axis=0)

    ones_bd = ones_ref[...]
    conv_out = _conv_group(z_ref, ubuf, conv_w_ref, conv_b_ref, ln_g_ref, ln_b_ref)
    pool_out = _pool_group(z_ref, pbuf, pool_w_ref, pool_scale_ref, start_pos + t * C)
    rwkv_out = _rwkv_group(z_ref, qprev, s_scr, ones_bd, mu_ref, w0_ref, w_up_ref, a0_ref,
                           a_up_ref, g_up_ref, k_k_ref, k_a_ref, r_k_ref, gn_g_ref, gn_b_ref)
    mix_ref[:, 0:CONV_DIM] = conv_out.astype(BF16)
    mix_ref[:, CONV_DIM:CONV_DIM + POOL_DIM] = pool_out.astype(BF16)
    mix_ref[:, CONV_DIM + POOL_DIM:] = rwkv_out.astype(BF16)

    ubuf[2:32, :] = ubuf[C + 2:C + 32, :]
    pbuf[1:16, :] = pbuf[C + 1:C + 16, :]

    @pl.when(t == pl.num_programs(1) - 1)
    def _():
        newc_ref[...] = ubuf[2:32, :]
        newp_ref[...] = pbuf[1:16, :]
        news_ref[...] = qprev[0:1, :]
        for p in range(PAIRS):
            sp = s_scr[p]
            news_wkv_ref[2 * p] = sp[0:HEAD_SIZE, 0:HEAD_SIZE]
            news_wkv_ref[2 * p + 1] = sp[HEAD_SIZE:, HEAD_SIZE:]


def _mixers(z, hist_conv, hist_pool, hist_shift, state_wkv, ones_bd, wts, layer, start_pos):
    B, L, _ = z.shape
    C = CHUNK
    assert L % C == 0
    (conv_w, conv_b, ln_g, ln_b, pool_w, pool_scale, mu, w0, w_up, a0, a_up, g_up,
     k_k, k_a, r_k, gn_g, gn_b) = wts

    def per_layer(shape):
        nd = len(shape)
        return pl.BlockSpec((None,) + shape, lambda b, t: (layer,) + (0,) * nd)

    def per_seq(shape):
        nd = len(shape)
        return pl.BlockSpec((None, None) + shape, lambda b, t: (layer, b) + (0,) * nd)

    def out_seq(shape):
        nd = len(shape)
        return pl.BlockSpec((None,) + shape, lambda b, t: (b,) + (0,) * nd)

    vec = lambda n: per_layer((1, n))
    in_specs = [
        pl.BlockSpec((None, C, IN_PROJ), lambda b, t: (b, t, 0)),
        per_seq((CONV_HIST, CONV_DIM)),
        per_seq((POOL_HIST, POOL_DIM)),
        per_seq((1, RWKV_PROJ)),
        per_seq((RWKV_HEADS, HEAD_SIZE, HEAD_SIZE)),
        pl.BlockSpec((LANES, LANES), lambda b, t: (0, 0)),
        per_layer((CONV_WIDTH, CONV_DIM)), vec(CONV_DIM), vec(CONV_DIM), vec(CONV_DIM),
        per_layer((len(POOL_WINDOWS), POOL_GROUP_DIM, POOL_GROUP_DIM)), vec(POOL_DIM),
        vec(RWKV_PROJ), vec(RWKV_DIM), per_layer((LORA, RWKV_DIM)), vec(RWKV_DIM),
        per_layer((LORA, RWKV_DIM)), per_layer((LORA, RWKV_DIM)),
        vec(RWKV_DIM), vec(RWKV_DIM), vec(RWKV_DIM), vec(RWKV_DIM), vec(RWKV_DIM),
    ]
    out_specs = [
        pl.BlockSpec((None, C, D_MODEL), lambda b, t: (b, t, 0)),
        out_seq((CONV_HIST, CONV_DIM)),
        out_seq((POOL_HIST, POOL_DIM)),
        out_seq((1, RWKV_PROJ)),
        out_seq((RWKV_HEADS, HEAD_SIZE, HEAD_SIZE)),
    ]
    out_shape = [
        jax.ShapeDtypeStruct((B, L, D_MODEL), BF16),
        jax.ShapeDtypeStruct((B, CONV_HIST, CONV_DIM), F32),
        jax.ShapeDtypeStruct((B, POOL_HIST, POOL_DIM), F32),
        jax.ShapeDtypeStruct((B, 1, RWKV_PROJ), F32),
        jax.ShapeDtypeStruct((B, RWKV_HEADS, HEAD_SIZE, HEAD_SIZE), F32),
    ]
    scratch = [
        pltpu.VMEM((32 + C, CONV_DIM), F32),
        pltpu.VMEM((16 + C, POOL_DIM), F32),
        pltpu.VMEM((8, RWKV_PROJ), F32),
        pltpu.VMEM((PAIRS, LANES, LANES), F32),
    ]
    return pl.pallas_call(
        functools.partial(_mixer_kernel, start_pos),
        grid=(B, L // C),
        in_specs=in_specs,
        out_specs=out_specs,
        out_shape=out_shape,
        scratch_shapes=scratch,
        compiler_params=pltpu.CompilerParams(
            dimension_semantics=("arbitrary", "arbitrary"), vmem_limit_bytes=VMEM_LIMIT),
    )(z, hist_conv, hist_pool, hist_shift, state_wkv, ones_bd,
      conv_w, conv_b, ln_g, ln_b, pool_w, pool_scale, mu, w0, w_up, a0, a_up, g_up,
      k_k, k_a, r_k, gn_g, gn_b)


def _trunk(x, hist_conv, hist_pool, hist_shift, state_wkv, start_pos, weights):
    (norm_mix, w_in, conv_w, conv_b, conv_ln_g, conv_ln_b, pool_w, pool_scale,
     shift_mu, decay_w0, decay_up, iclr_a0, iclr_up, gate_up, k_k, k_a, r_k, gn_g, gn_b,
     w_out, norm_ffn, ffn_gate, ffn_up, ffn_down, norm_final) = weights
    B, L, _ = x.shape
    T = B * L
    as_rows = lambda a: a.reshape(DEPTH, 1, a.shape[-1])
    mixer_wts = (conv_w, as_rows(conv_b), as_rows(conv_ln_g), as_rows(conv_ln_b), pool_w,
                 as_rows(pool_scale), as_rows(shift_mu), as_rows(decay_w0), decay_up,
                 as_rows(iclr_a0), iclr_up, gate_up, as_rows(k_k), as_rows(k_a),
                 as_rows(r_k.reshape(DEPTH, RWKV_DIM)), as_rows(gn_g), as_rows(gn_b))
    norm_mix3 = as_rows(norm_mix)
    norm_ffn3 = as_rows(norm_ffn)
    idx = jnp.arange(LANES) // HEAD_SIZE
    ones_bd = (idx[:, None] == idx[None, :]).astype(BF16)

    x = x.reshape(T, D_MODEL)
    convs, pools, shifts, wkvs = [], [], [], []
    for layer in range(DEPTH):
        z = _norm_in_proj(x, norm_mix3, w_in, layer)
        mix, c_new, p_new, s_new, S_new = _mixers(
            z.reshape(B, L, IN_PROJ), hist_conv, hist_pool, hist_shift, state_wkv, ones_bd,
            mixer_wts, layer, start_pos)
        x = _out_proj(x, mix.reshape(T, D_MODEL), w_out, layer)
        act = _ffn_up(x, norm_ffn3, ffn_gate, ffn_up, layer)
        x = _ffn_down(x, act, ffn_down, layer)
        convs.append(c_new)
        pools.append(p_new)
        shifts.append(s_new)
        wkvs.append(S_new)
    y = _final_norm(x, norm_final.reshape(1, D_MODEL)).reshape(B, L, D_MODEL)
    return y, jnp.stack(convs), jnp.stack(pools), jnp.stack(shifts), jnp.stack(wkvs)


def kernel(x_prompt, x_sample, cache_conv, cache_pool, state_shift, state_wkv, norm_mix, w_in,
           conv_w, conv_b, conv_ln_g, conv_ln_b, pool_w, pool_scale, shift_mu, decay_w0, decay_up,
           iclr_a0, iclr_up, gate_up, k_k, k_a, r_k, gn_g, gn_b, w_out, norm_ffn, ffn_gate,
           ffn_up, ffn_down, norm_final):
    weights = (norm_mix, w_in, conv_w, conv_b, conv_ln_g, conv_ln_b, pool_w, pool_scale,
               shift_mu, decay_w0, decay_up, iclr_a0, iclr_up, gate_up, k_k, k_a, r_k, gn_g, gn_b,
               w_out, norm_ffn, ffn_gate, ffn_up, ffn_down, norm_final)
    bp = x_prompt.shape[0]
    zc = jnp.zeros((DEPTH, bp, CONV_HIST, CONV_DIM), F32)
    zp = jnp.zeros((DEPTH, bp, POOL_HIST, POOL_DIM), F32)
    zs = jnp.zeros((DEPTH, bp, 1, RWKV_PROJ), F32)
    zw = jnp.zeros((DEPTH, bp, RWKV_HEADS, HEAD_SIZE, HEAD_SIZE), F32)
    y_p, p_conv, p_pool, p_shift, p_wkv = _trunk(x_prompt, zc, zp, zs, zw, 0, weights)
    y_s, s_conv, s_pool, s_shift, s_wkv = _trunk(x_sample, cache_conv, cache_pool, state_shift,
                                                 state_wkv, PAST_LEN, weights)
    return (y_p, y_s, p_conv, p_pool, p_shift, p_wkv, s_conv, s_pool, s_shift, s_wkv)
```

```python
import functools

import jax
import jax.numpy as jnp
from jax import lax
from jax.experimental import pallas as pl
from jax.experimental.pallas import tpu as pltpu

F32 = jnp.float32
BF16 = jnp.bfloat16

D_MODEL = 2048
DEPTH = 4
PAST_LEN = 4096
CONV_DIM = 512
CONV_WIDTH = 31
CONV_HIST = CONV_WIDTH - 1
POOL_DIM = 512
POOL_WINDOWS = (2, 4, 8, 16)
POOL_GROUP_DIM = 128
POOL_HIST = max(POOL_WINDOWS) - 1
RWKV_DIM = 1024
HEAD_SIZE = 64
RWKV_HEADS = RWKV_DIM // HEAD_SIZE
LORA = 64
RWKV_PROJ = 3 * RWKV_DIM + 3 * LORA
IN_PROJ = 2 * CONV_DIM + POOL_DIM + RWKV_PROJ
POOL_OFF = 2 * CONV_DIM
RWKV_OFF = POOL_OFF + POOL_DIM
D_FF = 5632
RMS_EPS = 1e-6
LN_EPS = 1e-5
GN_EPS = 64e-5

LANES = 128
PAIRS = RWKV_DIM // LANES
CHUNK = 64
SOLVE_SQUARINGS = 5
VMEM_LIMIT = 56 * 1024 * 1024


def _tiles(n_tokens):
    tm = min(n_tokens, 1024)
    assert n_tokens % tm == 0
    return dict(tm=tm, tn=512, tk_ff=1408)


def _bdot(a, b):
    return jnp.dot(a.astype(BF16), b.astype(BF16), preferred_element_type=F32)


def _bdot_nt(a, b):
    return lax.dot_general(a.astype(BF16), b.astype(BF16), (((1,), (1,)), ((), ())),
                           preferred_element_type=F32)


def _sigmoid(x):
    return 1.0 / (1.0 + jnp.exp(-x))


def _norm_in_kernel(x_ref, g_ref, w_ref, o_ref, h_ref):
    @pl.when(pl.program_id(1) == 0)
    def _():
        x = x_ref[...]
        ms = jnp.mean(x * x, axis=-1, keepdims=True)
        h_ref[...] = (x * lax.rsqrt(ms + RMS_EPS) * g_ref[...]).astype(BF16)

    o_ref[...] = jnp.dot(h_ref[...], w_ref[...].astype(BF16), preferred_element_type=F32)


def _norm_in_proj(x, g, w, layer):
    T = x.shape[0]
    t = _tiles(T)
    tm, tn = t["tm"], t["tn"]
    N = w.shape[-1]
    return pl.pallas_call(
        _norm_in_kernel,
        grid=(T // tm, pl.cdiv(N, tn)),
        in_specs=[
            pl.BlockSpec((tm, D_MODEL), lambda m, n: (m, 0)),
            pl.BlockSpec((None, 1, D_MODEL), lambda m, n: (layer, 0, 0)),
            pl.BlockSpec((None, D_MODEL, tn), lambda m, n: (layer, 0, n)),
        ],
        out_specs=pl.BlockSpec((tm, tn), lambda m, n: (m, n)),
        out_shape=jax.ShapeDtypeStruct((T, N), F32),
        scratch_shapes=[pltpu.VMEM((tm, D_MODEL), BF16)],
        compiler_params=pltpu.CompilerParams(
            dimension_semantics=("arbitrary", "arbitrary"), vmem_limit_bytes=VMEM_LIMIT),
    )(x, g, w)


def _out_kernel(x_ref, a_ref, w_ref, o_ref):
    o_ref[...] = x_ref[...] + jnp.dot(a_ref[...], w_ref[...].astype(BF16),
                                      preferred_element_type=F32)


def _out_proj(x, a, w, layer):
    T = x.shape[0]
    t = _tiles(T)
    tm, tn = t["tm"], t["tn"]
    K = a.shape[-1]
    return pl.pallas_call(
        _out_kernel,
        grid=(T // tm, D_MODEL // tn),
        in_specs=[
            pl.BlockSpec((tm, tn), lambda m, n: (m, n)),
            pl.BlockSpec((tm, K), lambda m, n: (m, 0)),
            pl.BlockSpec((None, K, tn), lambda m, n: (layer, 0, n)),
        ],
        out_specs=pl.BlockSpec((tm, tn), lambda m, n: (m, n)),
        out_shape=jax.ShapeDtypeStruct((T, D_MODEL), F32),
        compiler_params=pltpu.CompilerParams(
            dimension_semantics=("arbitrary", "arbitrary"), vmem_limit_bytes=VMEM_LIMIT),
    )(x, a, w)


def _ffn_up_kernel(x_ref, g_ref, wg_ref, wu_ref, o_ref, h_ref):
    @pl.when(pl.program_id(1) == 0)
    def _():
        x = x_ref[...]
        ms = jnp.mean(x * x, axis=-1, keepdims=True)
        h_ref[...] = (x * lax.rsqrt(ms + RMS_EPS) * g_ref[...]).astype(BF16)

    h = h_ref[...]
    gate = jnp.dot(h, wg_ref[...].astype(BF16), preferred_element_type=F32)
    up = jnp.dot(h, wu_ref[...].astype(BF16), preferred_element_type=F32)
    o_ref[...] = (gate * _sigmoid(gate) * up).astype(BF16)


def _ffn_up(x, g, wg, wu, layer):
    T = x.shape[0]
    t = _tiles(T)
    tm, tn = t["tm"], t["tn"]
    return pl.pallas_call(
        _ffn_up_kernel,
        grid=(T // tm, D_FF // tn),
        in_specs=[
            pl.BlockSpec((tm, D_MODEL), lambda m, n: (m, 0)),
            pl.BlockSpec((None, 1, D_MODEL), lambda m, n: (layer, 0, 0)),
            pl.BlockSpec((None, D_MODEL, tn), lambda m, n: (layer, 0, n)),
            pl.BlockSpec((None, D_MODEL, tn), lambda m, n: (layer, 0, n)),
        ],
        out_specs=pl.BlockSpec((tm, tn), lambda m, n: (m, n)),
        out_shape=jax.ShapeDtypeStruct((T, D_FF), BF16),
        scratch_shapes=[pltpu.VMEM((tm, D_MODEL), BF16)],
        compiler_params=pltpu.CompilerParams(
            dimension_semantics=("arbitrary", "arbitrary"), vmem_limit_bytes=VMEM_LIMIT),
    )(x, g, wg, wu)


def _ffn_down_kernel(x_ref, a_ref, w_ref, o_ref, acc_ref):
    k = pl.program_id(2)

    @pl.when(k == 0)
    def _():
        acc_ref[...] = x_ref[...]

    acc_ref[...] += jnp.dot(a_ref[...], w_ref[...].astype(BF16), preferred_element_type=F32)

    @pl.when(k == pl.num_programs(2) - 1)
    def _():
        o_ref[...] = acc_ref[...]


def _ffn_down(x, a, w, layer):
    T = x.shape[0]
    t = _tiles(T)
    tm, tn, tk = t["tm"], t["tn"], t["tk_ff"]
    return pl.pallas_call(
        _ffn_down_kernel,
        grid=(T // tm, D_MODEL // tn, D_FF // tk),
        in_specs=[
            pl.BlockSpec((tm, tn), lambda m, n, k: (m, n)),
            pl.BlockSpec((tm, tk), lambda m, n, k: (m, k)),
            pl.BlockSpec((None, tk, tn), lambda m, n, k: (layer, k, n)),
        ],
        out_specs=pl.BlockSpec((tm, tn), lambda m, n, k: (m, n)),
        out_shape=jax.ShapeDtypeStruct((T, D_MODEL), F32),
        scratch_shapes=[pltpu.VMEM((tm, tn), F32)],
        compiler_params=pltpu.CompilerParams(
            dimension_semantics=("arbitrary", "arbitrary", "arbitrary"),
            vmem_limit_bytes=VMEM_LIMIT),
    )(x, a, w)


def _final_norm_kernel(x_ref, g_ref, o_ref):
    x = x_ref[...]
    ms = jnp.mean(x * x, axis=-1, keepdims=True)
    o_ref[...] = x * lax.rsqrt(ms + RMS_EPS) * g_ref[...]


def _final_norm(x, g):
    T = x.shape[0]
    tm = min(T, 512)
    return pl.pallas_call(
        _final_norm_kernel,
        grid=(T // tm,),
        in_specs=[
            pl.BlockSpec((tm, D_MODEL), lambda m: (m, 0)),
            pl.BlockSpec((1, D_MODEL), lambda m: (0, 0)),
        ],
        out_specs=pl.BlockSpec((tm, D_MODEL), lambda m: (m, 0)),
        out_shape=jax.ShapeDtypeStruct((T, D_MODEL), F32),
        compiler_params=pltpu.CompilerParams(dimension_semantics=("arbitrary",)),
    )(x, g)


def _conv_group(z_ref, ubuf, conv_w_ref, conv_b_ref, ln_g_ref, ln_b_ref):
    C = CHUNK
    val = z_ref[:, 0:CONV_DIM]
    gate = z_ref[:, CONV_DIM:2 * CONV_DIM]
    ubuf[32:32 + C, :] = val * _sigmoid(gate)
    acc = jnp.zeros((C, CONV_DIM), F32) + conv_b_ref[...]
    for j in range(CONV_WIDTH):
        acc = acc + ubuf[2 + j:2 + j + C, :] * conv_w_ref[j:j + 1, :]
    mu = jnp.mean(acc, axis=-1, keepdims=True)
    cen = acc - mu
    var = jnp.mean(cen * cen, axis=-1, keepdims=True)
    hn = cen * lax.rsqrt(var + LN_EPS) * ln_g_ref[...] + ln_b_ref[...]
    return hn * _sigmoid(hn)


def _pool_group(z_ref, pbuf, pool_w_ref, pool_scale_ref, pos0):
    C = CHUNK
    pbuf[16:16 + C, :] = z_ref[:, POOL_OFF:POOL_OFF + POOL_DIM]
    pos = pos0 + lax.broadcasted_iota(jnp.int32, (C, POOL_GROUP_DIM), 0)
    outs = []
    for gi, w in enumerate(POOL_WINDOWS):
        lo, hi = gi * POOL_GROUP_DIM, (gi + 1) * POOL_GROUP_DIM
        tok = pbuf[16:16 + C, lo:hi]
        tot = tok
        for j in range(1, w):
            tot = tot + pbuf[16 - j:16 - j + C, lo:hi]
        cnt = jnp.minimum(w, pos + 1).astype(F32)
        d = tot / cnt - tok
        outs.append(_bdot(d, pool_w_ref[gi]))
    return jnp.concatenate(outs, axis=-1) * pool_scale_ref[...]


def _head_sums(x, ones_bd):
    stacked = jnp.concatenate([x[:, p * LANES:(p + 1) * LANES] for p in range(PAIRS)], axis=0)
    s = _bdot(stacked, ones_bd)
    C = x.shape[0]
    return jnp.concatenate([s[p * C:(p + 1) * C, :] for p in range(PAIRS)], axis=-1)


def _wkv_chunk(r, lw, k2, v, kk, a, s_ref, ones_bd):
    C = CHUNK
    row = lax.broadcasted_iota(jnp.int32, (C, C), 0)
    col = lax.broadcasted_iota(jnp.int32, (C, C), 1)
    tri = (col <= row).astype(BF16)
    lw_hi = lw.astype(BF16)
    rem = lw - lw_hi.astype(F32)
    lw_mid = rem.astype(BF16)
    lw_lo = (rem - lw_mid.astype(F32)).astype(BF16)
    cum = (jnp.dot(tri, lw_hi, preferred_element_type=F32)
           + jnp.dot(tri, lw_mid, preferred_element_type=F32)
           + jnp.dot(tri, lw_lo, preferred_element_type=F32))
    tot = cum[C - 1:C, :]
    g_inv = jnp.exp(-cum)
    g_end = jnp.exp(tot - cum)
    b = kk * a
    A_t = -kk * jnp.exp(cum - lw)
    R_t = r * jnp.exp(cum)
    B_t = b * g_inv
    K_t = k2 * g_inv
    B_e = b * g_end
    K_e = k2 * g_end
    g_tot = jnp.exp(tot)

    low2 = lax.broadcasted_iota(jnp.int32, (2 * C, LANES), 1) < HEAD_SIZE
    low1 = lax.broadcasted_iota(jnp.int32, (C, LANES), 1) < HEAD_SIZE
    r4 = lax.broadcasted_iota(jnp.int32, (C, 4 * C), 0)
    c4 = lax.broadcasted_iota(jnp.int32, (C, 4 * C), 1) & (C - 1)
    strict = c4 < r4
    incl = c4 <= r4
    rr = lax.broadcasted_iota(jnp.int32, (LANES, LANES), 0)
    cc = lax.broadcasted_iota(jnp.int32, (LANES, LANES), 1)
    same_head = (rr < HEAD_SIZE) == (cc < HEAD_SIZE)
    z_tile = jnp.zeros((C, LANES), BF16)
    z_rows = jnp.zeros((C, 2 * LANES), BF16)

    def masked_rows(top, vlow, vhigh):
        return jnp.concatenate([jnp.where(low1, top, 0.0).astype(BF16), vlow,
                                jnp.where(low1, 0.0, top).astype(BF16), vhigh], axis=0)

    pairs = range(PAIRS)
    sls = [slice(p * LANES, (p + 1) * LANES) for p in pairs]
    S0 = [s_ref[p] for p in pairs]
    v_low = [jnp.where(low1, v[:, sl], 0.0).astype(BF16) for sl in sls]
    v_high = [jnp.where(low1, 0.0, v[:, sl]).astype(BF16) for sl in sls]

    GG = []
    for p, sl in zip(pairs, sls):
        AR = jnp.concatenate([A_t[:, sl], R_t[:, sl]], axis=0)
        lhs = jnp.concatenate([jnp.where(low2, AR, 0.0), jnp.where(low2, 0.0, AR)], axis=0)
        rhs = jnp.concatenate([B_t[:, sl], K_t[:, sl], S0[p]], axis=0)
        GG.append(_bdot_nt(lhs, rhs))
    P = [jnp.where(strict, jnp.concatenate([g[0:C, 0:2 * C], g[2 * C:3 * C, 0:2 * C]], axis=-1), 0.0)
         .astype(BF16) for g in GG]
    M = [jnp.where(incl, jnp.concatenate([g[C:2 * C, 0:2 * C], g[3 * C:4 * C, 0:2 * C]], axis=-1), 0.0)
         .astype(BF16) for g in GG]
    U = [g[0:C, 2 * C:] + g[2 * C:3 * C, 2 * C:] for g in GG]
    AS_bot = [g[C:2 * C, 2 * C:] + g[3 * C:4 * C, 2 * C:] for g in GG]

    for level in range(SOLVE_SQUARINGS + 1):
        Z = [masked_rows(U[p], v_low[p], v_high[p]) for p in pairs]
        U = [U[p] + jnp.dot(P[p], Z[p], preferred_element_type=F32) for p in pairs]
        if level < SOLVE_SQUARINGS:
            sq = [jnp.concatenate([
                jnp.concatenate([P[p][:, 0:2 * C], z_tile], axis=-1), z_rows,
                jnp.concatenate([z_tile, P[p][:, 2 * C:]], axis=-1), z_rows], axis=0) for p in pairs]
            P = [jnp.dot(P[p], sq[p], preferred_element_type=F32).astype(BF16) for p in pairs]

    ys = []
    for p, sl in zip(pairs, sls):
        Z = masked_rows(U[p], v_low[p], v_high[p])
        ys.append(AS_bot[p] + jnp.dot(M[p], Z, preferred_element_type=F32))
        UV = jnp.concatenate([U[p], v[:, sl]], axis=0)
        BKe = jnp.concatenate([B_e[:, sl], K_e[:, sl]], axis=0)
        upd = _bdot(UV.T, BKe)
        s_ref[p] = S0[p] * g_tot[:, sl] + jnp.where(same_head, upd, 0.0)
    return jnp.concatenate(ys, axis=-1)


def _rwkv_group(z_ref, qprev, s_ref, ones_bd, mu_ref, w0_ref, w_up_ref, a0_ref, a_up_ref,
                g_up_ref, k_k_ref, k_a_ref, r_k_ref, gn_g_ref, gn_b_ref):
    C = CHUNK
    q = z_ref[:, RWKV_OFF:RWKV_OFF + RWKV_PROJ]
    rolled = pltpu.roll(q, 1, 0)
    first_row = lax.broadcasted_iota(jnp.int32, (C, RWKV_PROJ), 0) == 0
    q_prev = jnp.where(first_row, qprev[0:1, :], rolled)
    qs = q + (q_prev - q) * mu_ref[...]
    qprev[0:1, :] = q[C - 1:C, :]

    c0 = 3 * RWKV_DIM
    r = qs[:, 0:RWKV_DIM]
    k = qs[:, RWKV_DIM:2 * RWKV_DIM]
    v = qs[:, 2 * RWKV_DIM:c0]
    w_lo = qs[:, c0:c0 + LORA]
    a_lo = qs[:, c0 + LORA:c0 + 2 * LORA]
    g_lo = qs[:, c0 + 2 * LORA:c0 + 3 * LORA]

    wraw = w0_ref[...] + _bdot(jnp.tanh(w_lo), w_up_ref[...])
    neg = -wraw
    softplus = jnp.maximum(neg, 0.0) + jnp.log(1.0 + jnp.exp(-jnp.abs(neg)))
    lw = -jnp.exp(-softplus - 0.5)
    a = _sigmoid(a0_ref[...] + _bdot(a_lo, a_up_ref[...]))
    g = _bdot(_sigmoid(g_lo), g_up_ref[...])

    kk = k * k_k_ref[...]
    kk = kk * lax.rsqrt(jnp.maximum(_head_sums(kk * kk, ones_bd), 1e-24))
    k2 = k * (1.0 + (a - 1.0) * k_a_ref[...])

    y = _wkv_chunk(r, lw, k2, v, kk, a, s_ref, ones_bd)

    inv_n = 1.0 / HEAD_SIZE
    ym = _head_sums(y, ones_bd) * inv_n
    yc = y - ym
    yv = _head_sums(yc * yc, ones_bd) * inv_n
    yn = yc * lax.rsqrt(yv + GN_EPS) * gn_g_ref[...] + gn_b_ref[...]
    bonus = _head_sums(r * k2 * r_k_ref[...], ones_bd) * v
    return (yn + bonus) * g


def _mixer_kernel(start_pos, z_ref, hc_ref, hp_ref, hs_ref, s0_ref, ones_ref,
                  conv_w_ref, conv_b_ref, ln_g_ref, ln_b_ref, pool_w_ref, pool_scale_ref,
                  mu_ref, w0_ref, w_up_ref, a0_ref, a_up_ref, g_up_ref, k_k_ref, k_a_ref,
                  r_k_ref, gn_g_ref, gn_b_ref,
                  mix_ref, newc_ref, newp_ref, news_ref, news_wkv_ref,
                  ubuf, pbuf, qprev, s_scr):
    C = CHUNK
    t = pl.program_id(1)

    @pl.when(t == 0)
    def _():
        ubuf[0:2, :] = jnp.zeros((2, CONV_DIM), F32)
        ubuf[2:32, :] = hc_ref[...]
        pbuf[0:1, :] = jnp.zeros((1, POOL_DIM), F32)
        pbuf[1:16, :] = hp_ref[...]
        qprev[...] = jnp.zeros(qprev.shape, F32)
        qprev[0:1, :] = hs_ref[...]
        zero = jnp.zeros((HEAD_SIZE, HEAD_SIZE), F32)
        for p in range(PAIRS):
            top = jnp.concatenate([s0_ref[2 * p], zero], axis=-1)
            bot = jnp.concatenate([zero, s0_ref[2 * p + 1]], axis=-1)
            s_scr[p] = jnp.concatenate([top, bot], axis=0)

    ones_bd = ones_ref[...]
    conv_out = _conv_group(z_ref, ubuf, conv_w_ref, conv_b_ref, ln_g_ref, ln_b_ref)
    pool_out = _pool_group(z_ref, pbuf, pool_w_ref, pool_scale_ref, start_pos + t * C)
    rwkv_out = _rwkv_group(z_ref, qprev, s_scr, ones_bd, mu_ref, w0_ref, w_up_ref, a0_ref,
                           a_up_ref, g_up_ref, k_k_ref, k_a_ref, r_k_ref, gn_g_ref, gn_b_ref)
    mix_ref[:, 0:CONV_DIM] = conv_out.astype(BF16)
    mix_ref[:, CONV_DIM:CONV_DIM + POOL_DIM] = pool_out.astype(BF16)
    mix_ref[:, CONV_DIM + POOL_DIM:] = rwkv_out.astype(BF16)

    ubuf[2:32, :] = ubuf[C + 2:C + 32, :]
    pbuf[1:16, :] = pbuf[C + 1:C + 16, :]

    @pl.when(t == pl.num_programs(1) - 1)
    def _():
        newc_ref[...] = ubuf[2:32, :]
        newp_ref[...] = pbuf[1:16, :]
        news_ref[...] = qprev[0:1, :]
        for p in range(PAIRS):
            sp = s_scr[p]
            news_wkv_ref[2 * p] = sp[0:HEAD_SIZE, 0:HEAD_SIZE]
            news_wkv_ref[2 * p + 1] = sp[HEAD_SIZE:, HEAD_SIZE:]


def _mixers(z, hist_conv, hist_pool, hist_shift, state_wkv, ones_bd, wts, layer, start_pos):
    B, L, _ = z.shape
    C = CHUNK
    assert L % C == 0
    (conv_w, conv_b, ln_g, ln_b, pool_w, pool_scale, mu, w0, w_up, a0, a_up, g_up,
     k_k, k_a, r_k, gn_g, gn_b) = wts

    def per_layer(shape):
        nd = len(shape)
        return pl.BlockSpec((None,) + shape, lambda b, t: (layer,) + (0,) * nd)

    def per_seq(shape):
        nd = len(shape)
        return pl.BlockSpec((None, None) + shape, lambda b, t: (layer, b) + (0,) * nd)

    def out_seq(shape):
        nd = len(shape)
        return pl.BlockSpec((None,) + shape, lambda b, t: (b,) + (0,) * nd)

    vec = lambda n: per_layer((1, n))
    in_specs = [
        pl.BlockSpec((None, C, IN_PROJ), lambda b, t: (b, t, 0)),
        per_seq((CONV_HIST, CONV_DIM)),
        per_seq((POOL_HIST, POOL_DIM)),
        per_seq((1, RWKV_PROJ)),
        per_seq((RWKV_HEADS, HEAD_SIZE, HEAD_SIZE)),
        pl.BlockSpec((LANES, LANES), lambda b, t: (0, 0)),
        per_layer((CONV_WIDTH, CONV_DIM)), vec(CONV_DIM), vec(CONV_DIM), vec(CONV_DIM),
        per_layer((len(POOL_WINDOWS), POOL_GROUP_DIM, POOL_GROUP_DIM)), vec(POOL_DIM),
        vec(RWKV_PROJ), vec(RWKV_DIM), per_layer((LORA, RWKV_DIM)), vec(RWKV_DIM),
        per_layer((LORA, RWKV_DIM)), per_layer((LORA, RWKV_DIM)),
        vec(RWKV_DIM), vec(RWKV_DIM), vec(RWKV_DIM), vec(RWKV_DIM), vec(RWKV_DIM),
    ]
    out_specs = [
        pl.BlockSpec((None, C, D_MODEL), lambda b, t: (b, t, 0)),
        out_seq((CONV_HIST, CONV_DIM)),
        out_seq((POOL_HIST, POOL_DIM)),
        out_seq((1, RWKV_PROJ)),
        out_seq((RWKV_HEADS, HEAD_SIZE, HEAD_SIZE)),
    ]
    out_shape = [
        jax.ShapeDtypeStruct((B, L, D_MODEL), BF16),
        jax.ShapeDtypeStruct((B, CONV_HIST, CONV_DIM), F32),
        jax.ShapeDtypeStruct((B, POOL_HIST, POOL_DIM), F32),
        jax.ShapeDtypeStruct((B, 1, RWKV_PROJ), F32),
        jax.ShapeDtypeStruct((B, RWKV_HEADS, HEAD_SIZE, HEAD_SIZE), F32),
    ]
    scratch = [
        pltpu.VMEM((32 + C, CONV_DIM), F32),
        pltpu.VMEM((16 + C, POOL_DIM), F32),
        pltpu.VMEM((8, RWKV_PROJ), F32),
        pltpu.VMEM((PAIRS, LANES, LANES), F32),
    ]
    return pl.pallas_call(
        functools.partial(_mixer_kernel, start_pos),
        grid=(B, L // C),
        in_specs=in_specs,
        out_specs=out_specs,
        out_shape=out_shape,
        scratch_shapes=scratch,
        compiler_params=pltpu.CompilerParams(
            dimension_semantics=("arbitrary", "arbitrary"), vmem_limit_bytes=VMEM_LIMIT),
    )(z, hist_conv, hist_pool, hist_shift, state_wkv, ones_bd,
      conv_w, conv_b, ln_g, ln_b, pool_w, pool_scale, mu, w0, w_up, a0, a_up, g_up,
      k_k, k_a, r_k, gn_g, gn_b)


def _trunk(x, hist_conv, hist_pool, hist_shift, state_wkv, start_pos, weights):
    (norm_mix, w_in, conv_w, conv_b, conv_ln_g, conv_ln_b, pool_w, pool_scale,
     shift_mu, decay_w0, decay_up, iclr_a0, iclr_up, gate_up, k_k, k_a, r_k, gn_g, gn_b,
     w_out, norm_ffn, ffn_gate, ffn_up, ffn_down, norm_final) = weights
    B, L, _ = x.shape
    T = B * L
    as_rows = lambda a: a.reshape(DEPTH, 1, a.shape[-1])
    mixer_wts = (conv_w, as_rows(conv_b), as_rows(conv_ln_g), as_rows(conv_ln_b), pool_w,
                 as_rows(pool_scale), as_rows(shift_mu), as_rows(decay_w0), decay_up,
                 as_rows(iclr_a0), iclr_up, gate_up, as_rows(k_k), as_rows(k_a),
                 as_rows(r_k.reshape(DEPTH, RWKV_DIM)), as_rows(gn_g), as_rows(gn_b))
    norm_mix3 = as_rows(norm_mix)
    norm_ffn3 = as_rows(norm_ffn)
    idx = jnp.arange(LANES) // HEAD_SIZE
    ones_bd = (idx[:, None] == idx[None, :]).astype(BF16)

    x = x.reshape(T, D_MODEL)
    convs, pools, shifts, wkvs = [], [], [], []
    for layer in range(DEPTH):
        z = _norm_in_proj(x, norm_mix3, w_in, layer)
        mix, c_new, p_new, s_new, S_new = _mixers(
            z.reshape(B, L, IN_PROJ), hist_conv, hist_pool, hist_shift, state_wkv, ones_bd,
            mixer_wts, layer, start_pos)
        x = _out_proj(x, mix.reshape(T, D_MODEL), w_out, layer)
        act = _ffn_up(x, norm_ffn3, ffn_gate, ffn_up, layer)
        x = _ffn_down(x, act, ffn_down, layer)
        convs.append(c_new)
        pools.append(p_new)
        shifts.append(s_new)
        wkvs.append(S_new)
    y = _final_norm(x, norm_final.reshape(1, D_MODEL)).reshape(B, L, D_MODEL)
    return y, jnp.stack(convs), jnp.stack(pools), jnp.stack(shifts), jnp.stack(wkvs)


def kernel(x_prompt, x_sample, cache_conv, cache_pool, state_shift, state_wkv, norm_mix, w_in,
           conv_w, conv_b, conv_ln_g, conv_ln_b, pool_w, pool_scale, shift_mu, decay_w0, decay_up,
           iclr_a0, iclr_up, gate_up, k_k, k_a, r_k, gn_g, gn_b, w_out, norm_ffn, ffn_gate,
           ffn_up, ffn_down, norm_final):
    weights = (norm_mix, w_in, conv_w, conv_b, conv_ln_g, conv_ln_b, pool_w, pool_scale,
               shift_mu, decay_w0, decay_up, iclr_a0, iclr_up, gate_up, k_k, k_a, r_k, gn_g, gn_b,
               w_out, norm_ffn, ffn_gate, ffn_up, ffn_down, norm_final)
    bp = x_prompt.shape[0]
    zc = jnp.zeros((DEPTH, bp, CONV_HIST, CONV_DIM), F32)
    zp = jnp.zeros((DEPTH, bp, POOL_HIST, POOL_DIM), F32)
    zs = jnp.zeros((DEPTH, bp, 1, RWKV_PROJ), F32)
    zw = jnp.zeros((DEPTH, bp, RWKV_HEADS, HEAD_SIZE, HEAD_SIZE), F32)
    y_p, p_conv, p_pool, p_shift, p_wkv = _trunk(x_prompt, zc, zp, zs, zw, 0, weights)
    y_s, s_conv, s_pool, s_shift, s_wkv = _trunk(x_sample, cache_conv, cache_pool, state_shift,
                                                 state_wkv, PAST_LEN, weights)
    return (y_p, y_s, p_conv, p_pool, p_shift, p_wkv, s_conv, s_pool, s_shift, s_wkv)
```

```python
import collections
import functools

import jax
import jax.numpy as jnp
from jax import lax
from jax.experimental import pallas as pl
from jax.experimental.pallas import tpu as pltpu

F32 = jnp.float32
BF16 = jnp.bfloat16

D_MODEL = 2048
DEPTH = 4
PAST_LEN = 4096
CONV_DIM = 512
CONV_WIDTH = 31
CONV_HIST = CONV_WIDTH - 1
POOL_DIM = 512
POOL_WINDOWS = (2, 4, 8, 16)
POOL_GROUP_DIM = 128
POOL_HIST = max(POOL_WINDOWS) - 1
RWKV_DIM = 1024
HEAD_SIZE = 64
RWKV_HEADS = RWKV_DIM // HEAD_SIZE
LORA = 64
RWKV_PROJ = 3 * RWKV_DIM + 3 * LORA
IN_PROJ = 2 * CONV_DIM + POOL_DIM + RWKV_PROJ
POOL_OFF = 2 * CONV_DIM
RWKV_OFF = POOL_OFF + POOL_DIM
D_FF = 5632
RMS_EPS = 1e-6
LN_EPS = 1e-5
GN_EPS = 64e-5

LANES = 128
SUBLANES = 8
PAIRS = RWKV_DIM // LANES
SEQS_PER_STEP = 4
CHUNK = 64
SOLVE_SQUARINGS = 5
VMEM_LIMIT = 56 * 1024 * 1024


def _tiles(n_tokens):
    tm = min(n_tokens, 1024)
    assert n_tokens % tm == 0
    return dict(tm=tm, tn=1024, tn_ff=512, tk_ff=512)


IN_PROJ_PAD = -(-IN_PROJ // 1024) * 1024


def _cast_kernel(w_ref, o_ref):
    o_ref[...] = w_ref[...].astype(BF16)


def _cast_weight(w, rows):
    depth, K, N = w.shape
    return pl.pallas_call(
        _cast_kernel,
        grid=(depth, K // rows),
        in_specs=[pl.BlockSpec((None, rows, N), lambda l, k: (l, k, 0))],
        out_specs=pl.BlockSpec((None, rows, N), lambda l, k: (l, k, 0)),
        out_shape=jax.ShapeDtypeStruct(w.shape, BF16),
        compiler_params=pltpu.CompilerParams(dimension_semantics=("arbitrary", "arbitrary")),
    )(w)


def _transpose_cast_kernel(n_valid, w_ref, o_ref):
    rows = w_ref.shape[0]
    row = pl.program_id(1) * rows + lax.broadcasted_iota(jnp.int32, w_ref.shape, 0)
    o_ref[...] = jnp.where(row < n_valid, w_ref[...], 0.0).T.astype(BF16)


def _transpose_cast_weight(w_t, n_pad, cols):
    depth, N, K = w_t.shape
    return pl.pallas_call(
        functools.partial(_transpose_cast_kernel, N),
        grid=(depth, n_pad // cols),
        in_specs=[pl.BlockSpec((None, cols, K), lambda l, n: (l, n, 0))],
        out_specs=pl.BlockSpec((None, K, cols), lambda l, n: (l, 0, n)),
        out_shape=jax.ShapeDtypeStruct((depth, K, n_pad), BF16),
        compiler_params=pltpu.CompilerParams(dimension_semantics=("arbitrary", "arbitrary")),
    )(w_t)


def _bdot(a, b):
    return jnp.dot(a.astype(BF16), b.astype(BF16), preferred_element_type=F32)


def _bdot_nt(a, b):
    return lax.dot_general(a.astype(BF16), b.astype(BF16), (((1,), (1,)), ((), ())),
                           preferred_element_type=F32)


def _sigmoid(x):
    return 1.0 / (1.0 + jnp.exp(-x))


def _norm_in_kernel(x_ref, g_ref, w_ref, o_ref, h_ref):
    @pl.when(pl.program_id(1) == 0)
    def _():
        x = x_ref[...]
        ms = jnp.mean(x * x, axis=-1, keepdims=True)
        h_ref[...] = (x * lax.rsqrt(ms + RMS_EPS) * g_ref[...]).astype(BF16)

    o_ref[...] = jnp.dot(h_ref[...], w_ref[...], preferred_element_type=F32)


def _norm_in_proj(x, g, w, layer):
    T = x.shape[0]
    t = _tiles(T)
    tm, tn = t["tm"], t["tn"]
    N = IN_PROJ
    return pl.pallas_call(
        _norm_in_kernel,
        grid=(T // tm, w.shape[-1] // tn),
        in_specs=[
            pl.BlockSpec((tm, D_MODEL), lambda m, n: (m, 0)),
            pl.BlockSpec((None, 1, D_MODEL), lambda m, n: (layer, 0, 0)),
            pl.BlockSpec((None, D_MODEL, tn), lambda m, n: (layer, 0, n)),
        ],
        out_specs=pl.BlockSpec((tm, tn), lambda m, n: (m, n)),
        out_shape=jax.ShapeDtypeStruct((T, N), F32),
        scratch_shapes=[pltpu.VMEM((tm, D_MODEL), BF16)],
        compiler_params=pltpu.CompilerParams(
            dimension_semantics=("arbitrary", "arbitrary"), vmem_limit_bytes=VMEM_LIMIT),
    )(x, g, w)


def _out_kernel(x_ref, a_ref, w_ref, o_ref):
    o_ref[...] = x_ref[...] + jnp.dot(a_ref[...], w_ref[...], preferred_element_type=F32)


def _out_proj(x, a, w, layer):
    T = x.shape[0]
    t = _tiles(T)
    tm, tn = t["tm"], t["tn"]
    K = a.shape[-1]
    return pl.pallas_call(
        _out_kernel,
        grid=(T // tm, D_MODEL // tn),
        in_specs=[
            pl.BlockSpec((tm, tn), lambda m, n: (m, n)),
            pl.BlockSpec((tm, K), lambda m, n: (m, 0)),
            pl.BlockSpec((None, K, tn), lambda m, n: (layer, 0, n)),
        ],
        out_specs=pl.BlockSpec((tm, tn), lambda m, n: (m, n)),
        out_shape=jax.ShapeDtypeStruct((T, D_MODEL), F32),
        compiler_params=pltpu.CompilerParams(
            dimension_semantics=("arbitrary", "arbitrary"), vmem_limit_bytes=VMEM_LIMIT),
    )(x, a, w)


def _ffn_up_kernel(x_ref, g_ref, wg_ref, wu_ref, o_ref, h_ref):
    @pl.when(pl.program_id(1) == 0)
    def _():
        x = x_ref[...]
        ms = jnp.mean(x * x, axis=-1, keepdims=True)
        h_ref[...] = (x * lax.rsqrt(ms + RMS_EPS) * g_ref[...]).astype(BF16)

    h = h_ref[...]
    gate = jnp.dot(h, wg_ref[...].astype(BF16), preferred_element_type=F32)
    up = jnp.dot(h, wu_ref[...].astype(BF16), preferred_element_type=F32)
    o_ref[...] = (gate * _sigmoid(gate) * up).astype(BF16)


def _ffn_up(x, g, wg, wu, layer):
    T = x.shape[0]
    t = _tiles(T)
    tm, tn = t["tm"], t["tn_ff"]
    return pl.pallas_call(
        _ffn_up_kernel,
        grid=(T // tm, D_FF // tn),
        in_specs=[
            pl.BlockSpec((tm, D_MODEL), lambda m, n: (m, 0)),
            pl.BlockSpec((None, 1, D_MODEL), lambda m, n: (layer, 0, 0)),
            pl.BlockSpec((None, D_MODEL, tn), lambda m, n: (layer, 0, n)),
            pl.BlockSpec((None, D_MODEL, tn), lambda m, n: (layer, 0, n)),
        ],
        out_specs=pl.BlockSpec((tm, tn), lambda m, n: (m, n)),
        out_shape=jax.ShapeDtypeStruct((T, D_FF), BF16),
        scratch_shapes=[pltpu.VMEM((tm, D_MODEL), BF16)],
        compiler_params=pltpu.CompilerParams(
            dimension_semantics=("arbitrary", "arbitrary"), vmem_limit_bytes=VMEM_LIMIT),
    )(x, g, wg, wu)


def _ffn_down_kernel(x_ref, a_ref, w_ref, o_ref):
    @pl.when(pl.program_id(1) == 0)
    def _():
        o_ref[...] = x_ref[...]

    o_ref[...] += jnp.dot(a_ref[...], w_ref[...], preferred_element_type=F32)


def _ffn_down(x, a, w, layer):
    T = x.shape[0]
    t = _tiles(T)
    tm, tk = t["tm"], t["tk_ff"]
    return pl.pallas_call(
        _ffn_down_kernel,
        grid=(T // tm, D_FF // tk),
        in_specs=[
            pl.BlockSpec((tm, D_MODEL), lambda m, k: (m, 0)),
            pl.BlockSpec((tm, tk), lambda m, k: (m, k)),
            pl.BlockSpec((None, tk, D_MODEL), lambda m, k: (layer, k, 0)),
        ],
        out_specs=pl.BlockSpec((tm, D_MODEL), lambda m, k: (m, 0)),
        out_shape=jax.ShapeDtypeStruct((T, D_MODEL), F32),
        compiler_params=pltpu.CompilerParams(
            dimension_semantics=("arbitrary", "arbitrary"), vmem_limit_bytes=VMEM_LIMIT),
    )(x, a, w)


def _final_norm_kernel(x_ref, g_ref, o_ref):
    x = x_ref[...]
    ms = jnp.mean(x * x, axis=-1, keepdims=True)
    o_ref[...] = x * lax.rsqrt(ms + RMS_EPS) * g_ref[...]


def _final_norm(x, g):
    T = x.shape[0]
    tm = min(T, 512)
    return pl.pallas_call(
        _final_norm_kernel,
        grid=(T // tm,),
        in_specs=[
            pl.BlockSpec((tm, D_MODEL), lambda m: (m, 0)),
            pl.BlockSpec((1, D_MODEL), lambda m: (0, 0)),
        ],
        out_specs=pl.BlockSpec((tm, D_MODEL), lambda m: (m, 0)),
        out_shape=jax.ShapeDtypeStruct((T, D_MODEL), F32),
        compiler_params=pltpu.CompilerParams(dimension_semantics=("arbitrary",)),
    )(x, g)


class _Seq:
    def __init__(self, i, z_ref, ubuf, pbuf, qprev, s_scr, mix_ref):
        self.z = z_ref.at[i]
        self.ubuf = ubuf.at[i]
        self.pbuf = pbuf.at[i]
        self.qprev = qprev.at[i]
        self.s = s_scr.at[i]
        self.mix = mix_ref.at[i]
        self.v = {}


def _head_sums(x, ones_bd):
    stacked = jnp.concatenate([x[:, p * LANES:(p + 1) * LANES] for p in range(PAIRS)], axis=0)
    s = _bdot(stacked, ones_bd)
    C = x.shape[0]
    return jnp.concatenate([s[p * C:(p + 1) * C, :] for p in range(PAIRS)], axis=-1)


def _conv_pool_pieces(s, W, pos0):
    C = CHUNK
    v = s.v

    def glu():
        val = s.z[:, 0:CONV_DIM]
        gate = s.z[:, CONV_DIM:2 * CONV_DIM]
        s.ubuf[32:32 + C, :] = val * _sigmoid(gate)

    def taps(tile):
        def f():
            lanes = slice(tile * LANES, (tile + 1) * LANES)
            rows = 32 + C
            full = s.ubuf[:, lanes]
            shifted = [full] + [pltpu.roll(full, rows - k, 0) for k in range(1, SUBLANES)]
            acc = jnp.zeros((C, LANES), F32) + W.conv_b[:, lanes]
            for j in range(CONV_WIDTH):
                phase, base = (2 + j) % SUBLANES, (2 + j) // SUBLANES * SUBLANES
                acc = acc + shifted[phase][base:base + C, :] * W.conv_w[j:j + 1, lanes]
            v["conv%d" % tile] = acc
        return f

    def norm():
        acc = jnp.concatenate([v.pop("conv%d" % t) for t in range(CONV_DIM // LANES)], axis=-1)
        mu = jnp.mean(acc, axis=-1, keepdims=True)
        cen = acc - mu
        var = jnp.mean(cen * cen, axis=-1, keepdims=True)
        hn = cen * lax.rsqrt(var + LN_EPS) * W.ln_g[...] + W.ln_b[...]
        s.mix[:, 0:CONV_DIM] = (hn * _sigmoid(hn)).astype(BF16)
        s.ubuf[2:32, :] = s.ubuf[C + 2:C + 32, :]

    def pool():
        s.pbuf[16:16 + C, :] = s.z[:, POOL_OFF:POOL_OFF + POOL_DIM]
        pos = pos0 + lax.broadcasted_iota(jnp.int32, (C, POOL_GROUP_DIM), 0)
        outs = []
        for gi, w in enumerate(POOL_WINDOWS):
            lo, hi = gi * POOL_GROUP_DIM, (gi + 1) * POOL_GROUP_DIM
            tok = s.pbuf[16:16 + C, lo:hi]
            tot = tok
            for j in range(1, w):
                tot = tot + s.pbuf[16 - j:16 - j + C, lo:hi]
            cnt = jnp.minimum(w, pos + 1).astype(F32)
            d = tot / cnt - tok
            outs.append(_bdot(d, W.pool_w[gi]))
        out = jnp.concatenate(outs, axis=-1) * W.pool_scale[...]
        s.mix[:, CONV_DIM:CONV_DIM + POOL_DIM] = out.astype(BF16)
        s.pbuf[1:16, :] = s.pbuf[C + 1:C + 16, :]

    return [glu] + [taps(t) for t in range(CONV_DIM // LANES)] + [norm, pool]


def _rwkv_prep_pieces(s, W, ones_bd):
    C = CHUNK
    v = s.v

    def shift(name, off, width):
        def f():
            q = s.z[:, RWKV_OFF + off:RWKV_OFF + off + width]
            rolled = pltpu.roll(q, 1, 0)
            first_row = lax.broadcasted_iota(jnp.int32, (C, width), 0) == 0
            q_prev = jnp.where(first_row, s.qprev[0:1, off:off + width], rolled)
            v[name] = q + (q_prev - q) * W.mu[:, off:off + width]
            s.qprev[0:1, off:off + width] = q[C - 1:C, :]
        return f

    def lora():
        lo = v.pop("lo")
        w_lo, a_lo, g_lo = lo[:, 0:LORA], lo[:, LORA:2 * LORA], lo[:, 2 * LORA:3 * LORA]
        neg = -(W.w0[...] + _bdot(jnp.tanh(w_lo), W.w_up[...]))
        softplus = jnp.maximum(neg, 0.0) + jnp.log(1.0 + jnp.exp(-jnp.abs(neg)))
        v["lw"] = -jnp.exp(-softplus - 0.5)
        v["a"] = _sigmoid(W.a0[...] + _bdot(a_lo, W.a_up[...]))
        v["g"] = _bdot(_sigmoid(g_lo), W.g_up[...])

    def keys():
        k, a = v.pop("k"), v["a"]
        kk = k * W.k_k[...]
        kk = kk * lax.rsqrt(jnp.maximum(_head_sums(kk * kk, ones_bd), 1e-24))
        v["kk"] = kk
        v["k2"] = k * (1.0 + (a - 1.0) * W.k_a[...])

    def decay():
        lw = v.pop("lw")
        row = lax.broadcasted_iota(jnp.int32, (C, C), 0)
        col = lax.broadcasted_iota(jnp.int32, (C, C), 1)
        tri = (col <= row).astype(BF16)
        lw_hi = lw.astype(BF16)
        rem = lw - lw_hi.astype(F32)
        lw_mid = rem.astype(BF16)
        lw_lo = (rem - lw_mid.astype(F32)).astype(BF16)
        cum = (jnp.dot(tri, lw_hi, preferred_element_type=F32)
               + jnp.dot(tri, lw_mid, preferred_element_type=F32)
               + jnp.dot(tri, lw_lo, preferred_element_type=F32))
        tot = cum[C - 1:C, :]
        v["e_ex"] = jnp.exp(cum - lw)
        v["e_in"] = jnp.exp(cum)
        v["e_inv"] = jnp.exp(-cum)
        v["e_end"] = jnp.exp(tot - cum)
        v["g_tot"] = jnp.exp(tot)

    def products():
        kk, k2, a = v.pop("kk"), v["k2"], v.pop("a")
        b = kk * a
        e_inv, e_end = v.pop("e_inv"), v.pop("e_end")
        v["A_t"] = -kk * v.pop("e_ex")
        v["R_t"] = v["r"] * v.pop("e_in")
        v["B_t"] = b * e_inv
        v["K_t"] = k2 * e_inv
        v["B_e"] = b * e_end
        v["K_e"] = k2 * e_end

    return [shift("lo", 3 * RWKV_DIM, 3 * LORA), lora, shift("k", RWKV_DIM, RWKV_DIM), keys, decay,
            shift("r", 0, RWKV_DIM), shift("v", 2 * RWKV_DIM, RWKV_DIM), products]


def _wkv_stages(s):
    C = CHUNK
    v = s.v
    pairs = range(PAIRS)
    sl = [slice(p * LANES, (p + 1) * LANES) for p in pairs]
    low2 = lax.broadcasted_iota(jnp.int32, (2 * C, LANES), 1) < HEAD_SIZE
    low1 = lax.broadcasted_iota(jnp.int32, (C, LANES), 1) < HEAD_SIZE
    r4 = lax.broadcasted_iota(jnp.int32, (C, 4 * C), 0)
    c4 = lax.broadcasted_iota(jnp.int32, (C, 4 * C), 1) & (C - 1)
    strict = c4 < r4
    incl = c4 <= r4
    rr = lax.broadcasted_iota(jnp.int32, (LANES, LANES), 0)
    cc = lax.broadcasted_iota(jnp.int32, (LANES, LANES), 1)
    same_head = (rr < HEAD_SIZE) == (cc < HEAD_SIZE)
    z_tile = jnp.zeros((C, LANES), BF16)
    z_rows = jnp.zeros((C, 2 * LANES), BF16)
    w = {}

    def masked_rows(p):
        u = w["U"][p]
        return jnp.concatenate([jnp.where(low1, u, 0.0).astype(BF16), w["v_low"][p],
                                jnp.where(low1, 0.0, u).astype(BF16), w["v_high"][p]], axis=0)

    def gram():
        w["S0"] = [s.s[p] for p in pairs]
        w["v_low"] = [jnp.where(low1, v["v"][:, sl[p]], 0.0).astype(BF16) for p in pairs]
        w["v_high"] = [jnp.where(low1, 0.0, v["v"][:, sl[p]]).astype(BF16) for p in pairs]
        A_t, R_t, B_t, K_t = v.pop("A_t"), v.pop("R_t"), v.pop("B_t"), v.pop("K_t")
        GG = []
        for p in pairs:
            AR = jnp.concatenate([A_t[:, sl[p]], R_t[:, sl[p]]], axis=0)
            lhs = jnp.concatenate([jnp.where(low2, AR, 0.0), jnp.where(low2, 0.0, AR)], axis=0)
            rhs = jnp.concatenate([B_t[:, sl[p]], K_t[:, sl[p]], w["S0"][p]], axis=0)
            GG.append(_bdot_nt(lhs, rhs))
        side = lambda g, r0: jnp.concatenate(
            [g[r0:r0 + C, 0:2 * C], g[r0 + 2 * C:r0 + 3 * C, 0:2 * C]], axis=-1)
        w["P"] = [jnp.where(strict, side(g, 0), 0.0).astype(BF16) for g in GG]
        w["M"] = [jnp.where(incl, side(g, C), 0.0).astype(BF16) for g in GG]
        w["U"] = [g[0:C, 2 * C:] + g[2 * C:3 * C, 2 * C:] for g in GG]
        w["AS_bot"] = [g[C:2 * C, 2 * C:] + g[3 * C:4 * C, 2 * C:] for g in GG]

    def level(square):
        def f():
            Z = [masked_rows(p) for p in pairs]
            P = w["P"]
            w["U"] = [w["U"][p] + jnp.dot(P[p], Z[p], preferred_element_type=F32) for p in pairs]
            if square:
                sq = [jnp.concatenate([
                    jnp.concatenate([P[p][:, 0:2 * C], z_tile], axis=-1), z_rows,
                    jnp.concatenate([z_tile, P[p][:, 2 * C:]], axis=-1), z_rows], axis=0)
                    for p in pairs]
                w["P"] = [jnp.dot(P[p], sq[p], preferred_element_type=F32).astype(BF16)
                          for p in pairs]
        return f

    def finish():
        B_e, K_e, g_tot = v.pop("B_e"), v.pop("K_e"), v.pop("g_tot")
        ys = []
        for p in pairs:
            ys.append(w["AS_bot"][p]
                      + jnp.dot(w["M"][p], masked_rows(p), preferred_element_type=F32))
            UV = jnp.concatenate([w["U"][p], v["v"][:, sl[p]]], axis=0)
            BKe = jnp.concatenate([B_e[:, sl[p]], K_e[:, sl[p]]], axis=0)
            upd = _bdot(UV.T, BKe)
            s.s[p] = w["S0"][p] * g_tot[:, sl[p]] + jnp.where(same_head, upd, 0.0)
        v["y"] = jnp.concatenate(ys, axis=-1)
        w.clear()

    return [gram] + [level(True) for _ in range(SOLVE_SQUARINGS)] + [level(False), finish]


def _rwkv_post_pieces(s, W, ones_bd):
    v = s.v
    inv_n = 1.0 / HEAD_SIZE

    def center():
        y = v.pop("y")
        v["yc"] = y - _head_sums(y, ones_bd) * inv_n

    def scale():
        yc = v.pop("yc")
        yv = _head_sums(yc * yc, ones_bd) * inv_n
        v["yn"] = yc * lax.rsqrt(yv + GN_EPS) * W.gn_g[...] + W.gn_b[...]

    def bonus():
        v["bonus"] = _head_sums(v.pop("r") * v.pop("k2") * W.r_k[...], ones_bd) * v.pop("v")

    def gate():
        out = (v.pop("yn") + v.pop("bonus")) * v.pop("g")
        s.mix[:, CONV_DIM + POOL_DIM:] = out.astype(BF16)

    return [center, scale, bonus, gate]


def _run_interleaved(stages, fillers):
    n = len(stages)
    for i, stage in enumerate(stages):
        stage()
        for f in fillers[i * len(fillers) // n:(i + 1) * len(fillers) // n]:
            f()


_MixerWeights = collections.namedtuple(
    "_MixerWeights", "conv_w conv_b ln_g ln_b pool_w pool_scale mu w0 w_up a0 a_up g_up k_k k_a "
                     "r_k gn_g gn_b")


def _mixer_kernel(start_pos, z_ref, hc_ref, hp_ref, hs_ref, s0_ref, ones_ref, *rest):
    n_w = len(_MixerWeights._fields)
    W = _MixerWeights(*rest[:n_w])
    mix_ref, newc_ref, newp_ref, news_ref, news_wkv_ref, ubuf, pbuf, qprev, s_scr = rest[n_w:]
    C = CHUNK
    t = pl.program_id(1)

    @pl.when(t == 0)
    def _():
        zero = jnp.zeros((HEAD_SIZE, HEAD_SIZE), F32)
        for i in range(SEQS_PER_STEP):
            ubuf[i, 0:2, :] = jnp.zeros((2, CONV_DIM), F32)
            ubuf[i, 2:32, :] = hc_ref[i]
            pbuf[i, 0:1, :] = jnp.zeros((1, POOL_DIM), F32)
            pbuf[i, 1:16, :] = hp_ref[i]
            qprev[i] = jnp.zeros(qprev.shape[1:], F32)
            qprev[i, 0:1, :] = hs_ref[i]
            for p in range(PAIRS):
                top = jnp.concatenate([s0_ref[i, 2 * p], zero], axis=-1)
                bot = jnp.concatenate([zero, s0_ref[i, 2 * p + 1]], axis=-1)
                s_scr[i, p] = jnp.concatenate([top, bot], axis=0)

    ones_bd = ones_ref[...]
    pos0 = start_pos + t * C
    seqs = [_Seq(i, z_ref, ubuf, pbuf, qprev, s_scr, mix_ref) for i in range(SEQS_PER_STEP)]
    prep = [_rwkv_prep_pieces(s, W, ones_bd) for s in seqs]
    side = [_conv_pool_pieces(s, W, pos0) for s in seqs]
    post = [_rwkv_post_pieces(s, W, ones_bd) for s in seqs]
    for f in prep[0]:
        f()
    for i, s in enumerate(seqs):
        fillers = side[i] + (prep[i + 1] if i + 1 < len(seqs) else []) + (post[i - 1] if i else [])
        _run_interleaved(_wkv_stages(s), fillers)
    for f in post[-1]:
        f()

    @pl.when(t == pl.num_programs(1) - 1)
    def _():
        for i in range(SEQS_PER_STEP):
            newc_ref[i] = ubuf[i, 2:32, :]
            newp_ref[i] = pbuf[i, 1:16, :]
            news_ref[i] = qprev[i, 0:1, :]
            for p in range(PAIRS):
                sp = s_scr[i, p]
                news_wkv_ref[i, 2 * p] = sp[0:HEAD_SIZE, 0:HEAD_SIZE]
                news_wkv_ref[i, 2 * p + 1] = sp[HEAD_SIZE:, HEAD_SIZE:]


def _mixers(z, hist_conv, hist_pool, hist_shift, state_wkv, ones_bd, wts, layer, start_pos):
    B, L, _ = z.shape
    C, G = CHUNK, SEQS_PER_STEP
    assert L % C == 0 and B % G == 0

    def per_layer(shape):
        nd = len(shape)
        return pl.BlockSpec((None,) + shape, lambda b, t: (layer,) + (0,) * nd)

    def per_seq(shape):
        nd = len(shape)
        return pl.BlockSpec((None, G) + shape, lambda b, t: (layer, b) + (0,) * nd)

    def out_seq(shape):
        nd = len(shape)
        return pl.BlockSpec((G,) + shape, lambda b, t: (b,) + (0,) * nd)

    vec = lambda n: per_layer((1, n))
    in_specs = [
        pl.BlockSpec((G, C, IN_PROJ), lambda b, t: (b, t, 0)),
        per_seq((CONV_HIST, CONV_DIM)),
        per_seq((POOL_HIST, POOL_DIM)),
        per_seq((1, RWKV_PROJ)),
        per_seq((RWKV_HEADS, HEAD_SIZE, HEAD_SIZE)),
        pl.BlockSpec((LANES, LANES), lambda b, t: (0, 0)),
        per_layer((CONV_WIDTH, CONV_DIM)), vec(CONV_DIM), vec(CONV_DIM), vec(CONV_DIM),
        per_layer((len(POOL_WINDOWS), POOL_GROUP_DIM, POOL_GROUP_DIM)), vec(POOL_DIM),
        vec(RWKV_PROJ), vec(RWKV_DIM), per_layer((LORA, RWKV_DIM)), vec(RWKV_DIM),
        per_layer((LORA, RWKV_DIM)), per_layer((LORA, RWKV_DIM)),
        vec(RWKV_DIM), vec(RWKV_DIM), vec(RWKV_DIM), vec(RWKV_DIM), vec(RWKV_DIM),
    ]
    out_specs = [
        pl.BlockSpec((G, C, D_MODEL), lambda b, t: (b, t, 0)),
        out_seq((CONV_HIST, CONV_DIM)),
        out_seq((POOL_HIST, POOL_DIM)),
        out_seq((1, RWKV_PROJ)),
        out_seq((RWKV_HEADS, HEAD_SIZE, HEAD_SIZE)),
    ]
    out_shape = [
        jax.ShapeDtypeStruct((B, L, D_MODEL), BF16),
        jax.ShapeDtypeStruct((B, CONV_HIST, CONV_DIM), F32),
        jax.ShapeDtypeStruct((B, POOL_HIST, POOL_DIM), F32),
        jax.ShapeDtypeStruct((B, 1, RWKV_PROJ), F32),
        jax.ShapeDtypeStruct((B, RWKV_HEADS, HEAD_SIZE, HEAD_SIZE), F32),
    ]
    scratch = [
        pltpu.VMEM((G, 32 + C, CONV_DIM), F32),
        pltpu.VMEM((G, 16 + C, POOL_DIM), F32),
        pltpu.VMEM((G, SUBLANES, RWKV_PROJ), F32),
        pltpu.VMEM((G, PAIRS, LANES, LANES), F32),
    ]
    return pl.pallas_call(
        functools.partial(_mixer_kernel, start_pos),
        grid=(B // G, L // C),
        in_specs=in_specs,
        out_specs=out_specs,
        out_shape=out_shape,
        scratch_shapes=scratch,
        compiler_params=pltpu.CompilerParams(
            dimension_semantics=("arbitrary", "arbitrary"), vmem_limit_bytes=VMEM_LIMIT),
    )(z, hist_conv, hist_pool, hist_shift, state_wkv, ones_bd, *wts)


def _prepare_weights(weights):
    weights = list(weights)
    w_in, w_out, ffn_down = weights[1], weights[19], weights[23]
    weights[1] = _transpose_cast_weight(jnp.swapaxes(w_in, 1, 2), IN_PROJ_PAD, 1024)
    weights[19] = _cast_weight(w_out, 512)
    weights[23] = _cast_weight(ffn_down, 512)
    return tuple(weights)


def _trunk(x, hist_conv, hist_pool, hist_shift, state_wkv, start_pos, weights):
    (norm_mix, w_in, conv_w, conv_b, conv_ln_g, conv_ln_b, pool_w, pool_scale,
     shift_mu, decay_w0, decay_up, iclr_a0, iclr_up, gate_up, k_k, k_a, r_k, gn_g, gn_b,
     w_out, norm_ffn, ffn_gate, ffn_up, ffn_down, norm_final) = weights
    B, L, _ = x.shape
    T = B * L
    as_rows = lambda a: a.reshape(DEPTH, 1, a.shape[-1])
    mixer_wts = (conv_w, as_rows(conv_b), as_rows(conv_ln_g), as_rows(conv_ln_b), pool_w,
                 as_rows(pool_scale), as_rows(shift_mu), as_rows(decay_w0), decay_up,
                 as_rows(iclr_a0), iclr_up, gate_up, as_rows(k_k), as_rows(k_a),
                 as_rows(r_k.reshape(DEPTH, RWKV_DIM)), as_rows(gn_g), as_rows(gn_b))
    norm_mix3 = as_rows(norm_mix)
    norm_ffn3 = as_rows(norm_ffn)
    idx = jnp.arange(LANES) // HEAD_SIZE
    ones_bd = (idx[:, None] == idx[None, :]).astype(BF16)

    x = x.reshape(T, D_MODEL)
    convs, pools, shifts, wkvs = [], [], [], []
    for layer in range(DEPTH):
        z = _norm_in_proj(x, norm_mix3, w_in, layer)
        mix, c_new, p_new, s_new, S_new = _mixers(
            z.reshape(B, L, IN_PROJ), hist_conv, hist_pool, hist_shift, state_wkv, ones_bd,
            mixer_wts, layer, start_pos)
        x = _out_proj(x, mix.reshape(T, D_MODEL), w_out, layer)
        act = _ffn_up(x, norm_ffn3, ffn_gate, ffn_up, layer)
        x = _ffn_down(x, act, ffn_down, layer)
        convs.append(c_new)
        pools.append(p_new)
        shifts.append(s_new)
        wkvs.append(S_new)
    y = _final_norm(x, norm_final.reshape(1, D_MODEL)).reshape(B, L, D_MODEL)
    return y, jnp.stack(convs), jnp.stack(pools), jnp.stack(shifts), jnp.stack(wkvs)


def kernel(x_prompt, x_sample, cache_conv, cache_pool, state_shift, state_wkv, norm_mix, w_in,
           conv_w, conv_b, conv_ln_g, conv_ln_b, pool_w, pool_scale, shift_mu, decay_w0, decay_up,
           iclr_a0, iclr_up, gate_up, k_k, k_a, r_k, gn_g, gn_b, w_out, norm_ffn, ffn_gate,
           ffn_up, ffn_down, norm_final):
    weights = (norm_mix, w_in, conv_w, conv_b, conv_ln_g, conv_ln_b, pool_w, pool_scale,
               shift_mu, decay_w0, decay_up, iclr_a0, iclr_up, gate_up, k_k, k_a, r_k, gn_g, gn_b,
               w_out, norm_ffn, ffn_gate, ffn_up, ffn_down, norm_final)
    weights = _prepare_weights(weights)
    bp = x_prompt.shape[0]
    zc = jnp.zeros((DEPTH, bp, CONV_HIST, CONV_DIM), F32)
    zp = jnp.zeros((DEPTH, bp, POOL_HIST, POOL_DIM), F32)
    zs = jnp.zeros((DEPTH, bp, 1, RWKV_PROJ), F32)
    zw = jnp.zeros((DEPTH, bp, RWKV_HEADS, HEAD_SIZE, HEAD_SIZE), F32)
    y_p, p_conv, p_pool, p_shift, p_wkv = _trunk(x_prompt, zc, zp, zs, zw, 0, weights)
    y_s, s_conv, s_pool, s_shift, s_wkv = _trunk(x_sample, cache_conv, cache_pool, state_shift,
                                                 state_wkv, PAST_LEN, weights)
    return (y_p, y_s, p_conv, p_pool, p_shift, p_wkv, s_conv, s_pool, s_shift, s_wkv)
```

```python
import collections
import functools
import math

import jax
import jax.numpy as jnp
from jax import lax
from jax.experimental import pallas as pl
from jax.experimental.pallas import tpu as pltpu

F32 = jnp.float32
BF16 = jnp.bfloat16

D_MODEL = 2048
DEPTH = 4
PAST_LEN = 4096
CONV_DIM = 512
CONV_WIDTH = 31
CONV_HIST = CONV_WIDTH - 1
POOL_DIM = 512
POOL_WINDOWS = (2, 4, 8, 16)
POOL_GROUP_DIM = 128
POOL_HIST = max(POOL_WINDOWS) - 1
RWKV_DIM = 1024
HEAD_SIZE = 64
RWKV_HEADS = RWKV_DIM // HEAD_SIZE
LORA = 64
RWKV_PROJ = 3 * RWKV_DIM + 3 * LORA
IN_PROJ = 2 * CONV_DIM + POOL_DIM + RWKV_PROJ
POOL_OFF = 2 * CONV_DIM
RWKV_OFF = POOL_OFF + POOL_DIM
D_FF = 5632
RMS_EPS = 1e-6
DECAY_SCALE = math.exp(-0.5)
LN_EPS = 1e-5
GN_EPS = 64e-5

LANES = 128
SUBLANES = 8
PAIRS = RWKV_DIM // LANES
SEQS_PER_STEP = 4
CHUNK = 64
SOLVE_SQUARINGS = 5
VMEM_LIMIT = 56 * 1024 * 1024


def _tiles(n_tokens):
    tm = min(n_tokens, 1024)
    assert n_tokens % tm == 0
    return dict(tm=tm, tn=1024, tn_ff=512, tm_res=min(n_tokens, 512))


IN_PROJ_PAD = -(-IN_PROJ // 1024) * 1024


def _cast_kernel(w_ref, o_ref):
    o_ref[...] = w_ref[...].astype(BF16)


def _cast_weight(w, rows):
    depth, K, N = w.shape
    return pl.pallas_call(
        _cast_kernel,
        grid=(depth, K // rows),
        in_specs=[pl.BlockSpec((None, rows, N), lambda l, k: (l, k, 0))],
        out_specs=pl.BlockSpec((None, rows, N), lambda l, k: (l, k, 0)),
        out_shape=jax.ShapeDtypeStruct(w.shape, BF16),
        compiler_params=pltpu.CompilerParams(dimension_semantics=("arbitrary", "arbitrary")),
    )(w)


def _transpose_cast_kernel(n_valid, w_ref, o_ref):
    rows = w_ref.shape[0]
    row = pl.program_id(1) * rows + lax.broadcasted_iota(jnp.int32, w_ref.shape, 0)
    o_ref[...] = jnp.where(row < n_valid, w_ref[...], 0.0).T.astype(BF16)


def _transpose_cast_weight(w_t, n_pad, cols):
    depth, N, K = w_t.shape
    return pl.pallas_call(
        functools.partial(_transpose_cast_kernel, N),
        grid=(depth, n_pad // cols),
        in_specs=[pl.BlockSpec((None, cols, K), lambda l, n: (l, n, 0))],
        out_specs=pl.BlockSpec((None, K, cols), lambda l, n: (l, 0, n)),
        out_shape=jax.ShapeDtypeStruct((depth, K, n_pad), BF16),
        compiler_params=pltpu.CompilerParams(dimension_semantics=("arbitrary", "arbitrary")),
    )(w_t)


def _bdot(a, b):
    return jnp.dot(a.astype(BF16), b.astype(BF16), preferred_element_type=F32)


def _bdot_nt(a, b):
    return lax.dot_general(a.astype(BF16), b.astype(BF16), (((1,), (1,)), ((), ())),
                           preferred_element_type=F32)


def _sigmoid(x):
    return 1.0 / (1.0 + jnp.exp(-x))


def _rms_norm(x, gain):
    ms = jnp.mean(x * x, axis=-1, keepdims=True)
    return x * lax.rsqrt(ms + RMS_EPS) * gain


def _in_proj_kernel(fuse_norm, x_ref, g_ref, w_ref, o_ref, *scratch):
    if fuse_norm:
        h_ref, = scratch

        @pl.when(pl.program_id(1) == 0)
        def _():
            h_ref[...] = _rms_norm(x_ref[...], g_ref[...]).astype(BF16)

        h = h_ref[...]
    else:
        h = x_ref[...]
    o_ref[...] = jnp.dot(h, w_ref[...], preferred_element_type=F32)


def _in_proj(x, g, w, layer, fuse_norm):
    T = x.shape[0]
    t = _tiles(T)
    tm, tn = t["tm"], t["tn"]
    return pl.pallas_call(
        functools.partial(_in_proj_kernel, fuse_norm),
        grid=(T // tm, w.shape[-1] // tn),
        in_specs=[
            pl.BlockSpec((tm, D_MODEL), lambda m, n: (m, 0)),
            pl.BlockSpec((None, 1, D_MODEL), lambda m, n: (layer, 0, 0)),
            pl.BlockSpec((None, D_MODEL, tn), lambda m, n: (layer, 0, n)),
        ],
        out_specs=pl.BlockSpec((tm, tn), lambda m, n: (m, n)),
        out_shape=jax.ShapeDtypeStruct((T, IN_PROJ), F32),
        scratch_shapes=[pltpu.VMEM((tm, D_MODEL), BF16)] if fuse_norm else [],
        compiler_params=pltpu.CompilerParams(
            dimension_semantics=("arbitrary", "arbitrary"), vmem_limit_bytes=VMEM_LIMIT),
    )(x, g, w)


def _ffn_up_kernel(h_ref, wg_ref, wu_ref, o_ref):
    h = h_ref[...]
    gate = jnp.dot(h, wg_ref[...].astype(BF16), preferred_element_type=F32)
    up = jnp.dot(h, wu_ref[...].astype(BF16), preferred_element_type=F32)
    o_ref[...] = (gate * _sigmoid(gate) * up).astype(BF16)


def _ffn_up(h, wg, wu, layer):
    T = h.shape[0]
    t = _tiles(T)
    tm, tn = t["tm"], t["tn_ff"]
    return pl.pallas_call(
        _ffn_up_kernel,
        grid=(T // tm, D_FF // tn),
        in_specs=[
            pl.BlockSpec((tm, D_MODEL), lambda m, n: (m, 0)),
            pl.BlockSpec((None, D_MODEL, tn), lambda m, n: (layer, 0, n)),
            pl.BlockSpec((None, D_MODEL, tn), lambda m, n: (layer, 0, n)),
        ],
        out_specs=pl.BlockSpec((tm, tn), lambda m, n: (m, n)),
        out_shape=jax.ShapeDtypeStruct((T, D_FF), BF16),
        compiler_params=pltpu.CompilerParams(
            dimension_semantics=("arbitrary", "arbitrary"), vmem_limit_bytes=VMEM_LIMIT),
    )(h, wg, wu)


def _residual_proj_kernel(final, x_ref, a_ref, w_ref, g_ref, o_ref, *h_ref):
    k = pl.program_id(1)

    @pl.when(k == 0)
    def _():
        o_ref[...] = x_ref[...]

    o_ref[...] += jnp.dot(a_ref[...], w_ref[...], preferred_element_type=F32)

    @pl.when(k == pl.num_programs(1) - 1)
    def _():
        normed = _rms_norm(o_ref[...], g_ref[...])
        if final:
            o_ref[...] = normed
        else:
            h_ref[0][...] = normed.astype(BF16)


def _residual_proj(x, a, w, layer, gain, tk, final):
    T, K = a.shape
    tm = _tiles(T)["tm_res"]
    row_spec = pl.BlockSpec((tm, D_MODEL), lambda m, k: (m, 0))
    out_shape = [jax.ShapeDtypeStruct((T, D_MODEL), F32)]
    if not final:
        out_shape.append(jax.ShapeDtypeStruct((T, D_MODEL), BF16))
    out = pl.pallas_call(
        functools.partial(_residual_proj_kernel, final),
        grid=(T // tm, K // tk),
        in_specs=[
            row_spec,
            pl.BlockSpec((tm, tk), lambda m, k: (m, k)),
            pl.BlockSpec((None, tk, D_MODEL), lambda m, k: (layer, k, 0)),
            pl.BlockSpec((1, D_MODEL), lambda m, k: (0, 0)),
        ],
        out_specs=[row_spec] * len(out_shape),
        out_shape=out_shape,
        compiler_params=pltpu.CompilerParams(
            dimension_semantics=("arbitrary", "arbitrary"), vmem_limit_bytes=VMEM_LIMIT),
    )(x, a, w, gain)
    return out[0] if final else out


class _Seq:
    def __init__(self, i, z_ref, ubuf, pbuf, qprev, s_scr, mix_ref):
        self.z = z_ref.at[i]
        self.ubuf = ubuf.at[i]
        self.pbuf = pbuf.at[i]
        self.qprev = qprev.at[i]
        self.s = s_scr.at[i]
        self.mix = mix_ref.at[i]
        self.v = {}


def _head_sums(x, ones_bd):
    stacked = jnp.concatenate([x[:, p * LANES:(p + 1) * LANES] for p in range(PAIRS)], axis=0)
    s = _bdot(stacked, ones_bd)
    C = x.shape[0]
    return jnp.concatenate([s[p * C:(p + 1) * C, :] for p in range(PAIRS)], axis=-1)


def _lane_tile_sum(x):
    tiles = [x[:, i * LANES:(i + 1) * LANES] for i in range(x.shape[-1] // LANES)]
    while len(tiles) > 1:
        tiles = [a + b for a, b in zip(tiles[0::2], tiles[1::2])] + tiles[len(tiles) & ~1:]
    return tiles[0]


def _conv_pool_pieces(s, W, pos0):
    C = CHUNK
    v = s.v

    def glu():
        val = s.z[:, 0:CONV_DIM]
        gate = s.z[:, CONV_DIM:2 * CONV_DIM]
        s.ubuf[32:32 + C, :] = val * _sigmoid(gate)

    def taps(tile):
        def f():
            lanes = slice(tile * LANES, (tile + 1) * LANES)
            rows = 32 + C
            full = s.ubuf[:, lanes]
            shifted = [full] + [pltpu.roll(full, rows - k, 0) for k in range(1, SUBLANES)]
            acc = jnp.zeros((C, LANES), F32) + W.conv_b[:, lanes]
            for j in range(CONV_WIDTH):
                phase, base = (2 + j) % SUBLANES, (2 + j) // SUBLANES * SUBLANES
                acc = acc + shifted[phase][base:base + C, :] * W.conv_w[j:j + 1, lanes]
            v["conv%d" % tile] = acc
            return acc
        return f

    def norm():
        acc = jnp.concatenate([v.pop("conv%d" % t) for t in range(CONV_DIM // LANES)], axis=-1)
        mu = jnp.mean(acc, axis=-1, keepdims=True)
        cen = acc - mu
        var = jnp.mean(cen * cen, axis=-1, keepdims=True)
        hn = cen * lax.rsqrt(var + LN_EPS) * W.ln_g[...] + W.ln_b[...]
        out = hn * _sigmoid(hn)
        s.mix[:, 0:CONV_DIM] = out.astype(BF16)
        s.ubuf[2:32, :] = s.ubuf[C + 2:C + 32, :]
        return _lane_tile_sum(out)

    def pool():
        s.pbuf[16:16 + C, :] = s.z[:, POOL_OFF:POOL_OFF + POOL_DIM]
        pos = pos0 + lax.broadcasted_iota(jnp.int32, (C, POOL_GROUP_DIM), 0)
        outs = []
        for gi, w in enumerate(POOL_WINDOWS):
            lo, hi = gi * POOL_GROUP_DIM, (gi + 1) * POOL_GROUP_DIM
            tok = s.pbuf[16:16 + C, lo:hi]
            tot = tok
            for j in range(1, w):
                tot = tot + s.pbuf[16 - j:16 - j + C, lo:hi]
            cnt = jnp.minimum(w, pos + 1).astype(F32)
            d = tot / cnt - tok
            outs.append(_bdot(d, W.pool_w[gi]))
        out = jnp.concatenate(outs, axis=-1) * W.pool_scale[...]
        s.mix[:, CONV_DIM:CONV_DIM + POOL_DIM] = out.astype(BF16)
        s.pbuf[1:16, :] = s.pbuf[C + 1:C + 16, :]
        return _lane_tile_sum(out)

    return [glu] + [taps(t) for t in range(CONV_DIM // LANES)] + [norm, pool]


def _rwkv_prep_pieces(s, W, ones_bd):
    C = CHUNK
    v = s.v

    def shift(name, off, width):
        def f():
            q = s.z[:, RWKV_OFF + off:RWKV_OFF + off + width]
            rolled = pltpu.roll(q, 1, 0)
            first_row = lax.broadcasted_iota(jnp.int32, (C, width), 0) == 0
            q_prev = jnp.where(first_row, s.qprev[0:1, off:off + width], rolled)
            v[name] = q + (q_prev - q) * W.mu[:, off:off + width]
            s.qprev[0:1, off:off + width] = q[C - 1:C, :]
        return f

    def lora():
        lo = v.pop("lo")
        w_lo, a_lo, g_lo = lo[:, 0:LORA], lo[:, LORA:2 * LORA], lo[:, 2 * LORA:3 * LORA]
        x = W.w0[...] + _bdot(jnp.tanh(w_lo), W.w_up[...])
        v["lw"] = -DECAY_SCALE * _sigmoid(x)
        v["a"] = _sigmoid(W.a0[...] + _bdot(a_lo, W.a_up[...]))
        v["g"] = _bdot(_sigmoid(g_lo), W.g_up[...])

    def keys():
        k, a = v.pop("k"), v["a"]
        kk = k * W.k_k[...]
        kk = kk * lax.rsqrt(jnp.maximum(_head_sums(kk * kk, ones_bd), 1e-24))
        v["kk"] = kk
        v["k2"] = k * (1.0 + (a - 1.0) * W.k_a[...])

    def decay():
        lw = v.pop("lw")
        row = lax.broadcasted_iota(jnp.int32, (C, C), 0)
        col = lax.broadcasted_iota(jnp.int32, (C, C), 1)
        tri = (col <= row).astype(BF16)
        lw_hi = lw.astype(BF16)
        rem = lw - lw_hi.astype(F32)
        lw_mid = rem.astype(BF16)
        lw_lo = (rem - lw_mid.astype(F32)).astype(BF16)
        cum = (jnp.dot(tri, lw_hi, preferred_element_type=F32)
               + jnp.dot(tri, lw_mid, preferred_element_type=F32)
               + jnp.dot(tri, lw_lo, preferred_element_type=F32))
        tot = cum[C - 1:C, :]
        v["e_ex"] = jnp.exp(cum - lw)
        v["e_in"] = jnp.exp(cum)
        v["e_inv"] = jnp.exp(-cum)
        v["e_end"] = jnp.exp(tot - cum)
        v["g_tot"] = jnp.exp(tot)

    def products():
        kk, k2, a = v.pop("kk"), v["k2"], v.pop("a")
        b = kk * a
        e_inv, e_end = v.pop("e_inv"), v.pop("e_end")
        v["A_t"] = -kk * v.pop("e_ex")
        v["R_t"] = v["r"] * v.pop("e_in")
        v["B_t"] = b * e_inv
        v["K_t"] = k2 * e_inv
        v["B_e"] = b * e_end
        v["K_e"] = k2 * e_end

    return [shift("lo", 3 * RWKV_DIM, 3 * LORA), lora, shift("k", RWKV_DIM, RWKV_DIM), keys, decay,
            shift("r", 0, RWKV_DIM), shift("v", 2 * RWKV_DIM, RWKV_DIM), products]


def _wkv_stages(s, pins, never):
    C = CHUNK
    v = s.v
    pairs = range(PAIRS)
    sl = [slice(p * LANES, (p + 1) * LANES) for p in pairs]
    low2 = lax.broadcasted_iota(jnp.int32, (2 * C, LANES), 1) < HEAD_SIZE
    low1 = lax.broadcasted_iota(jnp.int32, (C, LANES), 1) < HEAD_SIZE
    r4 = lax.broadcasted_iota(jnp.int32, (C, 4 * C), 0)
    c4 = lax.broadcasted_iota(jnp.int32, (C, 4 * C), 1) & (C - 1)
    strict = c4 < r4
    incl = c4 <= r4
    rr = lax.broadcasted_iota(jnp.int32, (LANES, LANES), 0)
    cc = lax.broadcasted_iota(jnp.int32, (LANES, LANES), 1)
    same_head = (rr < HEAD_SIZE) == (cc < HEAD_SIZE)
    z_tile = jnp.zeros((C, LANES), BF16)
    z_rows = jnp.zeros((C, 2 * LANES), BF16)
    w = {}

    def masked_rows(p):
        u = w["U"][p]
        return jnp.concatenate([jnp.where(low1, u, 0.0).astype(BF16), w["v_low"][p],
                                jnp.where(low1, 0.0, u).astype(BF16), w["v_high"][p]], axis=0)

    def gram():
        w["S0"] = [s.s[p] for p in pairs]
        w["v_low"] = [jnp.where(low1, v["v"][:, sl[p]], 0.0).astype(BF16) for p in pairs]
        w["v_high"] = [jnp.where(low1, 0.0, v["v"][:, sl[p]]).astype(BF16) for p in pairs]
        A_t, R_t, B_t, K_t = v.pop("A_t"), v.pop("R_t"), v.pop("B_t"), v.pop("K_t")
        GG = []
        for p in pairs:
            AR = jnp.concatenate([A_t[:, sl[p]], R_t[:, sl[p]]], axis=0)
            lhs = jnp.concatenate([jnp.where(low2, AR, 0.0), jnp.where(low2, 0.0, AR)], axis=0)
            rhs = jnp.concatenate([B_t[:, sl[p]], K_t[:, sl[p]], w["S0"][p]], axis=0)
            GG.append(_bdot_nt(lhs, rhs))
        side = lambda g, r0: jnp.concatenate(
            [g[r0:r0 + C, 0:2 * C], g[r0 + 2 * C:r0 + 3 * C, 0:2 * C]], axis=-1)
        w["P"] = [jnp.where(strict, side(g, 0), 0.0).astype(BF16) for g in GG]
        w["M"] = [jnp.where(incl, side(g, C), 0.0).astype(BF16) for g in GG]
        w["U"] = [g[0:C, 2 * C:] + g[2 * C:3 * C, 2 * C:] for g in GG]
        w["AS_bot"] = [g[C:2 * C, 2 * C:] + g[3 * C:4 * C, 2 * C:] for g in GG]

    def take_pins():
        while pins:
            w["U"][PAIRS - 1] = jnp.where(never, pins.pop(), w["U"][PAIRS - 1])

    def level(square):
        def f():
            take_pins()
            Z = [masked_rows(p) for p in pairs]
            P = w["P"]
            w["U"] = [w["U"][p] + jnp.dot(P[p], Z[p], preferred_element_type=F32) for p in pairs]
            if square:
                sq = [jnp.concatenate([
                    jnp.concatenate([P[p][:, 0:2 * C], z_tile], axis=-1), z_rows,
                    jnp.concatenate([z_tile, P[p][:, 2 * C:]], axis=-1), z_rows], axis=0)
                    for p in pairs]
                w["P"] = [jnp.dot(P[p], sq[p], preferred_element_type=F32).astype(BF16)
                          for p in pairs]
        return f

    def finish():
        take_pins()
        B_e, K_e, g_tot = v.pop("B_e"), v.pop("K_e"), v.pop("g_tot")
        ys = []
        for p in pairs:
            ys.append(w["AS_bot"][p]
                      + jnp.dot(w["M"][p], masked_rows(p), preferred_element_type=F32))
            UV = jnp.concatenate([w["U"][p], v["v"][:, sl[p]]], axis=0)
            BKe = jnp.concatenate([B_e[:, sl[p]], K_e[:, sl[p]]], axis=0)
            upd = _bdot(UV.T, BKe)
            s.s[p] = w["S0"][p] * g_tot[:, sl[p]] + jnp.where(same_head, upd, 0.0)
        v["y"] = jnp.concatenate(ys, axis=-1)
        w.clear()

    return [gram] + [level(True) for _ in range(SOLVE_SQUARINGS)] + [level(False), finish]


def _rwkv_post_pieces(s, W, ones_bd):
    v = s.v
    inv_n = 1.0 / HEAD_SIZE

    def center():
        y = v.pop("y")
        v["yc"] = y - _head_sums(y, ones_bd) * inv_n

    def scale():
        yc = v.pop("yc")
        yv = _head_sums(yc * yc, ones_bd) * inv_n
        v["yn"] = yc * lax.rsqrt(yv + GN_EPS) * W.gn_g[...] + W.gn_b[...]

    def bonus():
        v["bonus"] = _head_sums(v.pop("r") * v.pop("k2") * W.r_k[...], ones_bd) * v.pop("v")

    def gate():
        out = (v.pop("yn") + v.pop("bonus")) * v.pop("g")
        s.mix[:, CONV_DIM + POOL_DIM:] = out.astype(BF16)
        return _lane_tile_sum(out)

    return [center, scale, bonus, gate]


def _run_interleaved(stages, fillers, pins):
    n = len(stages)
    for i, stage in enumerate(stages):
        stage()
        for f in fillers[i * len(fillers) // n:(i + 1) * len(fillers) // n]:
            token = f()
            if token is not None:
                pins.append(token)


_MixerWeights = collections.namedtuple(
    "_MixerWeights", "conv_w conv_b ln_g ln_b pool_w pool_scale mu w0 w_up a0 a_up g_up k_k k_a "
                     "r_k gn_g gn_b")


def _mixer_kernel(start_pos, z_ref, hc_ref, hp_ref, hs_ref, s0_ref, ones_ref, *rest):
    n_w = len(_MixerWeights._fields)
    W = _MixerWeights(*rest[:n_w])
    mix_ref, newc_ref, newp_ref, news_ref, news_wkv_ref, ubuf, pbuf, qprev, s_scr = rest[n_w:]
    C = CHUNK
    t = pl.program_id(1)

    @pl.when(t == 0)
    def _():
        zero = jnp.zeros((HEAD_SIZE, HEAD_SIZE), F32)
        for i in range(SEQS_PER_STEP):
            ubuf[i, 0:2, :] = jnp.zeros((2, CONV_DIM), F32)
            ubuf[i, 2:32, :] = hc_ref[i]
            pbuf[i, 0:1, :] = jnp.zeros((1, POOL_DIM), F32)
            pbuf[i, 1:16, :] = hp_ref[i]
            qprev[i] = jnp.zeros(qprev.shape[1:], F32)
            qprev[i, 0:1, :] = hs_ref[i]
            for p in range(PAIRS):
                top = jnp.concatenate([s0_ref[i, 2 * p], zero], axis=-1)
                bot = jnp.concatenate([zero, s0_ref[i, 2 * p + 1]], axis=-1)
                s_scr[i, p] = jnp.concatenate([top, bot], axis=0)

    ones_bd = ones_ref[...]
    pos0 = start_pos + t * C
    seqs = [_Seq(i, z_ref, ubuf, pbuf, qprev, s_scr, mix_ref) for i in range(SEQS_PER_STEP)]
    prep = [_rwkv_prep_pieces(s, W, ones_bd) for s in seqs]
    side = [_conv_pool_pieces(s, W, pos0) for s in seqs]
    post = [_rwkv_post_pieces(s, W, ones_bd) for s in seqs]
    pins = []
    never = t < 0
    for f in prep[0]:
        f()
    for i, s in enumerate(seqs):
        fillers = side[i] + (prep[i + 1] if i + 1 < len(seqs) else []) + (post[i - 1] if i else [])
        _run_interleaved(_wkv_stages(s, pins, never), fillers, pins)
    for f in post[-1]:
        f()

    @pl.when(t == pl.num_programs(1) - 1)
    def _():
        for i in range(SEQS_PER_STEP):
            newc_ref[i] = ubuf[i, 2:32, :]
            newp_ref[i] = pbuf[i, 1:16, :]
            news_ref[i] = qprev[i, 0:1, :]
            for p in range(PAIRS):
                sp = s_scr[i, p]
                news_wkv_ref[i, 2 * p] = sp[0:HEAD_SIZE, 0:HEAD_SIZE]
                news_wkv_ref[i, 2 * p + 1] = sp[HEAD_SIZE:, HEAD_SIZE:]


def _mixers(z, hist_conv, hist_pool, hist_shift, state_wkv, ones_bd, wts, layer, start_pos):
    B, L, _ = z.shape
    C, G = CHUNK, SEQS_PER_STEP
    assert L % C == 0 and B % G == 0

    def per_layer(shape):
        nd = len(shape)
        return pl.BlockSpec((None,) + shape, lambda b, t: (layer,) + (0,) * nd)

    def per_seq(shape):
        nd = len(shape)
        return pl.BlockSpec((None, G) + shape, lambda b, t: (layer, b) + (0,) * nd)

    def out_seq(shape):
        nd = len(shape)
        return pl.BlockSpec((G,) + shape, lambda b, t: (b,) + (0,) * nd)

    vec = lambda n: per_layer((1, n))
    in_specs = [
        pl.BlockSpec((G, C, IN_PROJ), lambda b, t: (b, t, 0)),
        per_seq((CONV_HIST, CONV_DIM)),
        per_seq((POOL_HIST, POOL_DIM)),
        per_seq((1, RWKV_PROJ)),
        per_seq((RWKV_HEADS, HEAD_SIZE, HEAD_SIZE)),
        pl.BlockSpec((LANES, LANES), lambda b, t: (0, 0)),
        per_layer((CONV_WIDTH, CONV_DIM)), vec(CONV_DIM), vec(CONV_DIM), vec(CONV_DIM),
        per_layer((len(POOL_WINDOWS), POOL_GROUP_DIM, POOL_GROUP_DIM)), vec(POOL_DIM),
        vec(RWKV_PROJ), vec(RWKV_DIM), per_layer((LORA, RWKV_DIM)), vec(RWKV_DIM),
        per_layer((LORA, RWKV_DIM)), per_layer((LORA, RWKV_DIM)),
        vec(RWKV_DIM), vec(RWKV_DIM), vec(RWKV_DIM), vec(RWKV_DIM), vec(RWKV_DIM),
    ]
    out_specs = [
        pl.BlockSpec((G, C, D_MODEL), lambda b, t: (b, t, 0)),
        out_seq((CONV_HIST, CONV_DIM)),
        out_seq((POOL_HIST, POOL_DIM)),
        out_seq((1, RWKV_PROJ)),
        out_seq((RWKV_HEADS, HEAD_SIZE, HEAD_SIZE)),
    ]
    out_shape = [
        jax.ShapeDtypeStruct((B, L, D_MODEL), BF16),
        jax.ShapeDtypeStruct((B, CONV_HIST, CONV_DIM), F32),
        jax.ShapeDtypeStruct((B, POOL_HIST, POOL_DIM), F32),
        jax.ShapeDtypeStruct((B, 1, RWKV_PROJ), F32),
        jax.ShapeDtypeStruct((B, RWKV_HEADS, HEAD_SIZE, HEAD_SIZE), F32),
    ]
    scratch = [
        pltpu.VMEM((G, 32 + C, CONV_DIM), F32),
        pltpu.VMEM((G, 16 + C, POOL_DIM), F32),
        pltpu.VMEM((G, SUBLANES, RWKV_PROJ), F32),
        pltpu.VMEM((G, PAIRS, LANES, LANES), F32),
    ]
    return pl.pallas_call(
        functools.partial(_mixer_kernel, start_pos),
        grid=(B // G, L // C),
        in_specs=in_specs,
        out_specs=out_specs,
        out_shape=out_shape,
        scratch_shapes=scratch,
        compiler_params=pltpu.CompilerParams(
            dimension_semantics=("arbitrary", "arbitrary"), vmem_limit_bytes=VMEM_LIMIT),
    )(z, hist_conv, hist_pool, hist_shift, state_wkv, ones_bd, *wts)


def _prepare_weights(weights):
    weights = list(weights)
    w_in, w_out, ffn_down = weights[1], weights[19], weights[23]
    weights[1] = _transpose_cast_weight(jnp.swapaxes(w_in, 1, 2), IN_PROJ_PAD, 1024)
    weights[19] = _cast_weight(w_out, 512)
    weights[23] = _cast_weight(ffn_down, 512)
    return tuple(weights)


def _trunk(x, hist_conv, hist_pool, hist_shift, state_wkv, start_pos, weights):
    (norm_mix, w_in, conv_w, conv_b, conv_ln_g, conv_ln_b, pool_w, pool_scale,
     shift_mu, decay_w0, decay_up, iclr_a0, iclr_up, gate_up, k_k, k_a, r_k, gn_g, gn_b,
     w_out, norm_ffn, ffn_gate, ffn_up, ffn_down, norm_final) = weights
    B, L, _ = x.shape
    T = B * L
    as_rows = lambda a: a.reshape(DEPTH, 1, a.shape[-1])
    mixer_wts = (conv_w, as_rows(conv_b), as_rows(conv_ln_g), as_rows(conv_ln_b), pool_w,
                 as_rows(pool_scale), as_rows(shift_mu), as_rows(decay_w0), decay_up,
                 as_rows(iclr_a0), iclr_up, gate_up, as_rows(k_k), as_rows(k_a),
                 as_rows(r_k.reshape(DEPTH, RWKV_DIM)), as_rows(gn_g), as_rows(gn_b))
    norm_mix3 = as_rows(norm_mix)
    idx = jnp.arange(LANES) // HEAD_SIZE
    ones_bd = (idx[:, None] == idx[None, :]).astype(BF16)

    x = x.reshape(T, D_MODEL)
    h = x
    convs, pools, shifts, wkvs = [], [], [], []
    for layer in range(DEPTH):
        last = layer == DEPTH - 1
        z = _in_proj(h, norm_mix3, w_in, layer, fuse_norm=layer == 0)
        mix, c_new, p_new, s_new, S_new = _mixers(
            z.reshape(B, L, IN_PROJ), hist_conv, hist_pool, hist_shift, state_wkv, ones_bd,
            mixer_wts, layer, start_pos)
        x, h = _residual_proj(x, mix.reshape(T, D_MODEL), w_out, layer, norm_ffn[layer][None],
                              tk=1024, final=False)
        act = _ffn_up(h, ffn_gate, ffn_up, layer)
        gain = norm_final[None] if last else norm_mix[layer + 1][None]
        out = _residual_proj(x, act, ffn_down, layer, gain, tk=1408, final=last)
        if not last:
            x, h = out
        convs.append(c_new)
        pools.append(p_new)
        shifts.append(s_new)
        wkvs.append(S_new)
    y = out.reshape(B, L, D_MODEL)
    return y, jnp.stack(convs), jnp.stack(pools), jnp.stack(shifts), jnp.stack(wkvs)


def kernel(x_prompt, x_sample, cache_conv, cache_pool, state_shift, state_wkv, norm_mix, w_in,
           conv_w, conv_b, conv_ln_g, conv_ln_b, pool_w, pool_scale, shift_mu, decay_w0, decay_up,
           iclr_a0, iclr_up, gate_up, k_k, k_a, r_k, gn_g, gn_b, w_out, norm_ffn, ffn_gate,
           ffn_up, ffn_down, norm_final):
    weights = (norm_mix, w_in, conv_w, conv_b, conv_ln_g, conv_ln_b, pool_w, pool_scale,
               shift_mu, decay_w0, decay_up, iclr_a0, iclr_up, gate_up, k_k, k_a, r_k, gn_g, gn_b,
               w_out, norm_ffn, ffn_gate, ffn_up, ffn_down, norm_final)
    weights = _prepare_weights(weights)
    bp = x_prompt.shape[0]
    zc = jnp.zeros((DEPTH, bp, CONV_HIST, CONV_DIM), F32)
    zp = jnp.zeros((DEPTH, bp, POOL_HIST, POOL_DIM), F32)
    zs = jnp.zeros((DEPTH, bp, 1, RWKV_PROJ), F32)
    zw = jnp.zeros((DEPTH, bp, RWKV_HEADS, HEAD_SIZE, HEAD_SIZE), F32)
    y_p, p_conv, p_pool, p_shift, p_wkv = _trunk(x_prompt, zc, zp, zs, zw, 0, weights)
    y_s, s_conv, s_pool, s_shift, s_wkv = _trunk(x_sample, cache_conv, cache_pool, state_shift,
                                                 state_wkv, PAST_LEN, weights)
    return (y_p, y_s, p_conv, p_pool, p_shift, p_wkv, s_conv, s_pool, s_shift, s_wkv)
```

```python
import collections
import functools
import math

import jax
import jax.numpy as jnp
from jax import lax
from jax.experimental import pallas as pl
from jax.experimental.pallas import tpu as pltpu

F32 = jnp.float32
BF16 = jnp.bfloat16

D_MODEL = 2048
DEPTH = 4
PAST_LEN = 4096
CONV_DIM = 512
CONV_WIDTH = 31
CONV_HIST = CONV_WIDTH - 1
POOL_DIM = 512
POOL_WINDOWS = (2, 4, 8, 16)
POOL_GROUP_DIM = 128
POOL_HIST = max(POOL_WINDOWS) - 1
RWKV_DIM = 1024
HEAD_SIZE = 64
RWKV_HEADS = RWKV_DIM // HEAD_SIZE
LORA = 64
RWKV_PROJ = 3 * RWKV_DIM + 3 * LORA
IN_PROJ = 2 * CONV_DIM + POOL_DIM + RWKV_PROJ
POOL_OFF = 2 * CONV_DIM
RWKV_OFF = POOL_OFF + POOL_DIM
D_FF = 5632
RMS_EPS = 1e-6
DECAY_SCALE = math.exp(-0.5)
LN_EPS = 1e-5
GN_EPS = 64e-5

LANES = 128
SUBLANES = 8
PAIRS = RWKV_DIM // LANES
SEQS_PER_STEP = 4
SEQS_PER_ROUND = 2
CHUNK = 64
SOLVE_SQUARINGS = 5
VMEM_LIMIT = 56 * 1024 * 1024


def _tiles(n_tokens):
    tm = min(n_tokens, 1024)
    assert n_tokens % tm == 0
    return dict(tm=tm, tn=1024, tn_ff=512)


IN_PROJ_PAD = -(-IN_PROJ // 1024) * 1024


def _cast_kernel(w_ref, o_ref):
    o_ref[...] = w_ref[...].astype(BF16)


def _cast_weight(w, rows):
    depth, K, N = w.shape
    return pl.pallas_call(
        _cast_kernel,
        grid=(depth, K // rows),
        in_specs=[pl.BlockSpec((None, rows, N), lambda l, k: (l, k, 0))],
        out_specs=pl.BlockSpec((None, rows, N), lambda l, k: (l, k, 0)),
        out_shape=jax.ShapeDtypeStruct(w.shape, BF16),
        compiler_params=pltpu.CompilerParams(dimension_semantics=("arbitrary", "arbitrary")),
    )(w)


def _transpose_cast_kernel(n_valid, w_ref, o_ref):
    rows = w_ref.shape[0]
    row = pl.program_id(1) * rows + lax.broadcasted_iota(jnp.int32, w_ref.shape, 0)
    o_ref[...] = jnp.where(row < n_valid, w_ref[...], 0.0).T.astype(BF16)


def _transpose_cast_weight(w_t, n_pad, cols):
    depth, N, K = w_t.shape
    return pl.pallas_call(
        functools.partial(_transpose_cast_kernel, N),
        grid=(depth, n_pad // cols),
        in_specs=[pl.BlockSpec((None, cols, K), lambda l, n: (l, n, 0))],
        out_specs=pl.BlockSpec((None, K, cols), lambda l, n: (l, 0, n)),
        out_shape=jax.ShapeDtypeStruct((depth, K, n_pad), BF16),
        compiler_params=pltpu.CompilerParams(dimension_semantics=("arbitrary", "arbitrary")),
    )(w_t)


def _bdot(a, b):
    return jnp.dot(a.astype(BF16), b.astype(BF16), preferred_element_type=F32)


def _bdot_nt(a, b):
    return lax.dot_general(a.astype(BF16), b.astype(BF16), (((1,), (1,)), ((), ())),
                           preferred_element_type=F32)


def _sigmoid(x):
    return 1.0 / (1.0 + jnp.exp(-x))


def _rms_norm(x, gain):
    ms = jnp.mean(x * x, axis=-1, keepdims=True)
    return x * lax.rsqrt(ms + RMS_EPS) * gain


def _in_proj_kernel(fuse_norm, x_ref, g_ref, w_ref, o_ref, *scratch):
    if fuse_norm:
        h_ref, = scratch

        @pl.when(pl.program_id(1) == 0)
        def _():
            h_ref[...] = _rms_norm(x_ref[...], g_ref[...]).astype(BF16)

        h = h_ref[...]
    else:
        h = x_ref[...]
    o_ref[...] = jnp.dot(h, w_ref[...], preferred_element_type=F32)


def _in_proj(x, g, w, layer, fuse_norm):
    T = x.shape[0]
    t = _tiles(T)
    tm, tn = t["tm"], t["tn"]
    return pl.pallas_call(
        functools.partial(_in_proj_kernel, fuse_norm),
        grid=(T // tm, w.shape[-1] // tn),
        in_specs=[
            pl.BlockSpec((tm, D_MODEL), lambda m, n: (m, 0)),
            pl.BlockSpec((None, 1, D_MODEL), lambda m, n: (layer, 0, 0)),
            pl.BlockSpec((None, D_MODEL, tn), lambda m, n: (layer, 0, n)),
        ],
        out_specs=pl.BlockSpec((tm, tn), lambda m, n: (m, n)),
        out_shape=jax.ShapeDtypeStruct((T, IN_PROJ), F32),
        scratch_shapes=[pltpu.VMEM((tm, D_MODEL), BF16)] if fuse_norm else [],
        compiler_params=pltpu.CompilerParams(
            dimension_semantics=("arbitrary", "arbitrary"), vmem_limit_bytes=VMEM_LIMIT),
    )(x, g, w)


def _ffn_up_kernel(h_ref, wg_ref, wu_ref, o_ref):
    h = h_ref[...]
    gate = jnp.dot(h, wg_ref[...].astype(BF16), preferred_element_type=F32)
    up = jnp.dot(h, wu_ref[...].astype(BF16), preferred_element_type=F32)
    o_ref[...] = (gate * _sigmoid(gate) * up).astype(BF16)


def _ffn_up(h, wg, wu, layer):
    T = h.shape[0]
    t = _tiles(T)
    tm, tn = t["tm"], t["tn_ff"]
    return pl.pallas_call(
        _ffn_up_kernel,
        grid=(T // tm, D_FF // tn),
        in_specs=[
            pl.BlockSpec((tm, D_MODEL), lambda m, n: (m, 0)),
            pl.BlockSpec((None, D_MODEL, tn), lambda m, n: (layer, 0, n)),
            pl.BlockSpec((None, D_MODEL, tn), lambda m, n: (layer, 0, n)),
        ],
        out_specs=pl.BlockSpec((tm, tn), lambda m, n: (m, n)),
        out_shape=jax.ShapeDtypeStruct((T, D_FF), BF16),
        compiler_params=pltpu.CompilerParams(
            dimension_semantics=("arbitrary", "arbitrary"), vmem_limit_bytes=VMEM_LIMIT),
    )(h, wg, wu)


def _residual_proj_kernel(final, x_hbm, a_ref, w_ref, g_ref, o_ref, *rest):
    sem = rest[-1]
    m, k = pl.program_id(0), pl.program_id(1)
    tm = o_ref.shape[0]
    load_x = pltpu.make_async_copy(x_hbm.at[pl.ds(m * tm, tm), :], o_ref, sem)

    @pl.when(k == 0)
    def _():
        load_x.start()
        part = jnp.dot(a_ref[...], w_ref[...], preferred_element_type=F32)
        load_x.wait()
        o_ref[...] += part

    @pl.when(k != 0)
    def _():
        o_ref[...] += jnp.dot(a_ref[...], w_ref[...], preferred_element_type=F32)

    @pl.when(k == pl.num_programs(1) - 1)
    def _():
        normed = _rms_norm(o_ref[...], g_ref[...])
        if final:
            o_ref[...] = normed
        else:
            rest[0][...] = normed.astype(BF16)


def _residual_proj(x, a, w, layer, gain, tk, final):
    T, K = a.shape
    tm = _tiles(T)["tm"]
    row_spec = pl.BlockSpec((tm, D_MODEL), lambda m, k: (m, 0))
    out_shape = [jax.ShapeDtypeStruct((T, D_MODEL), F32)]
    if not final:
        out_shape.append(jax.ShapeDtypeStruct((T, D_MODEL), BF16))
    out = pl.pallas_call(
        functools.partial(_residual_proj_kernel, final),
        grid=(T // tm, K // tk),
        in_specs=[
            pl.BlockSpec(memory_space=pl.ANY),
            pl.BlockSpec((tm, tk), lambda m, k: (m, k)),
            pl.BlockSpec((None, tk, D_MODEL), lambda m, k: (layer, k, 0)),
            pl.BlockSpec((1, D_MODEL), lambda m, k: (0, 0)),
        ],
        out_specs=[row_spec] * len(out_shape),
        out_shape=out_shape,
        scratch_shapes=[pltpu.SemaphoreType.DMA(())],
        compiler_params=pltpu.CompilerParams(
            dimension_semantics=("arbitrary", "arbitrary"), vmem_limit_bytes=VMEM_LIMIT),
    )(x, a, w, gain)
    return out[0] if final else out


class _Seq:
    def __init__(self, i, z_ref, ubuf, pbuf, qprev, s_scr, mix_ref):
        self.z = z_ref.at[i]
        self.ubuf = ubuf.at[i]
        self.pbuf = pbuf.at[i]
        self.qprev = qprev.at[i]
        self.s = s_scr.at[i]
        self.mix = mix_ref.at[i]
        self.v = {}


def _head_sums(x, ones_bd):
    stacked = jnp.concatenate([x[:, p * LANES:(p + 1) * LANES] for p in range(PAIRS)], axis=0)
    s = _bdot(stacked, ones_bd)
    C = x.shape[0]
    return jnp.concatenate([s[p * C:(p + 1) * C, :] for p in range(PAIRS)], axis=-1)


def _lane_tile_sum(x):
    tiles = [x[:, i * LANES:(i + 1) * LANES] for i in range(x.shape[-1] // LANES)]
    while len(tiles) > 1:
        tiles = [a + b for a, b in zip(tiles[0::2], tiles[1::2])] + tiles[len(tiles) & ~1:]
    return tiles[0]


def _conv_pool_pieces(s, W, pos0):
    C = CHUNK
    v = s.v

    def glu():
        val = s.z[:, 0:CONV_DIM]
        gate = s.z[:, CONV_DIM:2 * CONV_DIM]
        s.ubuf[32:32 + C, :] = val * _sigmoid(gate)

    def taps(tile):
        def f():
            lanes = slice(tile * LANES, (tile + 1) * LANES)
            rows = 32 + C
            full = s.ubuf[:, lanes]
            shifted = [full] + [pltpu.roll(full, rows - k, 0) for k in range(1, SUBLANES)]
            acc = jnp.zeros((C, LANES), F32) + W.conv_b[:, lanes]
            for j in range(CONV_WIDTH):
                phase, base = (2 + j) % SUBLANES, (2 + j) // SUBLANES * SUBLANES
                acc = acc + shifted[phase][base:base + C, :] * W.conv_w[j:j + 1, lanes]
            v["conv%d" % tile] = acc
            return acc
        return f

    def norm():
        acc = jnp.concatenate([v.pop("conv%d" % t) for t in range(CONV_DIM // LANES)], axis=-1)
        mu = jnp.mean(acc, axis=-1, keepdims=True)
        cen = acc - mu
        var = jnp.mean(cen * cen, axis=-1, keepdims=True)
        hn = cen * lax.rsqrt(var + LN_EPS) * W.ln_g[...] + W.ln_b[...]
        out = hn * _sigmoid(hn)
        s.mix[:, 0:CONV_DIM] = out.astype(BF16)
        s.ubuf[2:32, :] = s.ubuf[C + 2:C + 32, :]
        return _lane_tile_sum(out)

    def pool():
        s.pbuf[16:16 + C, :] = s.z[:, POOL_OFF:POOL_OFF + POOL_DIM]
        pos = pos0 + lax.broadcasted_iota(jnp.int32, (C, POOL_GROUP_DIM), 0)
        outs = []
        for gi, w in enumerate(POOL_WINDOWS):
            lo, hi = gi * POOL_GROUP_DIM, (gi + 1) * POOL_GROUP_DIM
            tok = s.pbuf[16:16 + C, lo:hi]
            tot = tok
            for j in range(1, w):
                tot = tot + s.pbuf[16 - j:16 - j + C, lo:hi]
            cnt = jnp.minimum(w, pos + 1).astype(F32)
            d = tot / cnt - tok
            outs.append(_bdot(d, W.pool_w[gi]))
        out = jnp.concatenate(outs, axis=-1) * W.pool_scale[...]
        s.mix[:, CONV_DIM:CONV_DIM + POOL_DIM] = out.astype(BF16)
        s.pbuf[1:16, :] = s.pbuf[C + 1:C + 16, :]
        return _lane_tile_sum(out)

    return [glu] + [taps(t) for t in range(CONV_DIM // LANES)] + [norm, pool]


def _rwkv_prep_pieces(s, W, ones_bd):
    C = CHUNK
    v = s.v

    def shift(name, off, width):
        def f():
            q = s.z[:, RWKV_OFF + off:RWKV_OFF + off + width]
            rolled = pltpu.roll(q, 1, 0)
            first_row = lax.broadcasted_iota(jnp.int32, (C, width), 0) == 0
            q_prev = jnp.where(first_row, s.qprev[0:1, off:off + width], rolled)
            v[name] = q + (q_prev - q) * W.mu[:, off:off + width]
            s.qprev[0:1, off:off + width] = q[C - 1:C, :]
        return f

    def lora():
        lo = v.pop("lo")
        w_lo, a_lo, g_lo = lo[:, 0:LORA], lo[:, LORA:2 * LORA], lo[:, 2 * LORA:3 * LORA]
        x = W.w0[...] + _bdot(jnp.tanh(w_lo), W.w_up[...])
        v["lw"] = -DECAY_SCALE * _sigmoid(x)
        v["a"] = _sigmoid(W.a0[...] + _bdot(a_lo, W.a_up[...]))
        v["g"] = _bdot(_sigmoid(g_lo), W.g_up[...])

    def keys():
        k, a = v.pop("k"), v["a"]
        kk = k * W.k_k[...]
        kk = kk * lax.rsqrt(jnp.maximum(_head_sums(kk * kk, ones_bd), 1e-24))
        v["kk"] = kk
        v["k2"] = k * (1.0 + (a - 1.0) * W.k_a[...])

    def decay():
        lw = v.pop("lw")
        row = lax.broadcasted_iota(jnp.int32, (C, C), 0)
        col = lax.broadcasted_iota(jnp.int32, (C, C), 1)
        tri = (col <= row).astype(BF16)
        lw_hi = lw.astype(BF16)
        rem = lw - lw_hi.astype(F32)
        lw_mid = rem.astype(BF16)
        lw_lo = (rem - lw_mid.astype(F32)).astype(BF16)
        cum = (jnp.dot(tri, lw_hi, preferred_element_type=F32)
               + jnp.dot(tri, lw_mid, preferred_element_type=F32)
               + jnp.dot(tri, lw_lo, preferred_element_type=F32))
        tot = cum[C - 1:C, :]
        v["e_ex"] = jnp.exp(cum - lw)
        v["e_in"] = jnp.exp(cum)
        v["e_inv"] = jnp.exp(-cum)
        v["e_end"] = jnp.exp(tot - cum)
        v["g_tot"] = jnp.exp(tot)

    def products():
        kk, k2, a = v.pop("kk"), v["k2"], v.pop("a")
        b = kk * a
        e_inv, e_end = v.pop("e_inv"), v.pop("e_end")
        v["A_t"] = -kk * v.pop("e_ex")
        v["R_t"] = v["r"] * v.pop("e_in")
        v["B_t"] = b * e_inv
        v["K_t"] = k2 * e_inv
        v["B_e"] = b * e_end
        v["K_e"] = k2 * e_end

    return [shift("lo", 3 * RWKV_DIM, 3 * LORA), lora, shift("k", RWKV_DIM, RWKV_DIM), keys, decay,
            shift("r", 0, RWKV_DIM), shift("v", 2 * RWKV_DIM, RWKV_DIM), products]


def _wkv_stages(s, pins, never):
    C = CHUNK
    v = s.v
    pairs = range(PAIRS)
    sl = [slice(p * LANES, (p + 1) * LANES) for p in pairs]
    low2 = lax.broadcasted_iota(jnp.int32, (2 * C, LANES), 1) < HEAD_SIZE
    low1 = lax.broadcasted_iota(jnp.int32, (C, LANES), 1) < HEAD_SIZE
    r4 = lax.broadcasted_iota(jnp.int32, (C, 4 * C), 0)
    c4 = lax.broadcasted_iota(jnp.int32, (C, 4 * C), 1) & (C - 1)
    strict = c4 < r4
    incl = c4 <= r4
    rr = lax.broadcasted_iota(jnp.int32, (LANES, LANES), 0)
    cc = lax.broadcasted_iota(jnp.int32, (LANES, LANES), 1)
    same_head = (rr < HEAD_SIZE) == (cc < HEAD_SIZE)
    z_tile = jnp.zeros((C, LANES), BF16)
    z_rows = jnp.zeros((C, 2 * LANES), BF16)
    w = {}

    def masked_rows(p):
        u = w["U"][p]
        return jnp.concatenate([jnp.where(low1, u, 0.0).astype(BF16), w["v_low"][p],
                                jnp.where(low1, 0.0, u).astype(BF16), w["v_high"][p]], axis=0)

    def gram():
        w["S0"] = [s.s[p] for p in pairs]
        w["v_low"] = [jnp.where(low1, v["v"][:, sl[p]], 0.0).astype(BF16) for p in pairs]
        w["v_high"] = [jnp.where(low1, 0.0, v["v"][:, sl[p]]).astype(BF16) for p in pairs]
        A_t, R_t, B_t, K_t = v.pop("A_t"), v.pop("R_t"), v.pop("B_t"), v.pop("K_t")
        GG = []
        for p in pairs:
            AR = jnp.concatenate([A_t[:, sl[p]], R_t[:, sl[p]]], axis=0)
            lhs = jnp.concatenate([jnp.where(low2, AR, 0.0), jnp.where(low2, 0.0, AR)], axis=0)
            rhs = jnp.concatenate([B_t[:, sl[p]], K_t[:, sl[p]], w["S0"][p]], axis=0)
            GG.append(_bdot_nt(lhs, rhs))
        side = lambda g, r0: jnp.concatenate(
            [g[r0:r0 + C, 0:2 * C], g[r0 + 2 * C:r0 + 3 * C, 0:2 * C]], axis=-1)
        w["P"] = [jnp.where(strict, side(g, 0), 0.0).astype(BF16) for g in GG]
        w["M"] = [jnp.where(incl, side(g, C), 0.0).astype(BF16) for g in GG]
        w["U"] = [g[0:C, 2 * C:] + g[2 * C:3 * C, 2 * C:] for g in GG]
        w["AS_bot"] = [g[C:2 * C, 2 * C:] + g[3 * C:4 * C, 2 * C:] for g in GG]

    def take_pins():
        while pins:
            w["U"][PAIRS - 1] = jnp.where(never, pins.pop(), w["U"][PAIRS - 1])

    def level(square):
        def f():
            take_pins()
            Z = [masked_rows(p) for p in pairs]
            P = w["P"]
            w["U"] = [w["U"][p] + jnp.dot(P[p], Z[p], preferred_element_type=F32) for p in pairs]
            if square:
                sq = [jnp.concatenate([
                    jnp.concatenate([P[p][:, 0:2 * C], z_tile], axis=-1), z_rows,
                    jnp.concatenate([z_tile, P[p][:, 2 * C:]], axis=-1), z_rows], axis=0)
                    for p in pairs]
                w["P"] = [jnp.dot(P[p], sq[p], preferred_element_type=F32).astype(BF16)
                          for p in pairs]
        return f

    def finish():
        take_pins()
        B_e, K_e, g_tot = v.pop("B_e"), v.pop("K_e"), v.pop("g_tot")
        ys = []
        for p in pairs:
            ys.append(w["AS_bot"][p]
                      + jnp.dot(w["M"][p], masked_rows(p), preferred_element_type=F32))
            UV = jnp.concatenate([w["U"][p], v["v"][:, sl[p]]], axis=0)
            BKe = jnp.concatenate([B_e[:, sl[p]], K_e[:, sl[p]]], axis=0)
            upd = _bdot(UV.T, BKe)
            s.s[p] = w["S0"][p] * g_tot[:, sl[p]] + jnp.where(same_head, upd, 0.0)
        v["y"] = jnp.concatenate(ys, axis=-1)
        w.clear()

    return [gram] + [level(True) for _ in range(SOLVE_SQUARINGS)] + [level(False), finish]


def _rwkv_post_pieces(s, W, ones_bd):
    v = s.v
    inv_n = 1.0 / HEAD_SIZE

    def center():
        y = v.pop("y")
        v["yc"] = y - _head_sums(y, ones_bd) * inv_n

    def scale():
        yc = v.pop("yc")
        yv = _head_sums(yc * yc, ones_bd) * inv_n
        v["yn"] = yc * lax.rsqrt(yv + GN_EPS) * W.gn_g[...] + W.gn_b[...]

    def bonus():
        v["bonus"] = _head_sums(v.pop("r") * v.pop("k2") * W.r_k[...], ones_bd) * v.pop("v")

    def gate():
        out = (v.pop("yn") + v.pop("bonus")) * v.pop("g")
        s.mix[:, CONV_DIM + POOL_DIM:] = out.astype(BF16)
        return _lane_tile_sum(out)

    return [center, scale, bonus, gate]


def _run_interleaved(stages, fillers, pins):
    n = len(stages)
    for i, stage in enumerate(stages):
        stage()
        for f in fillers[i * len(fillers) // n:(i + 1) * len(fillers) // n]:
            token = f()
            if token is not None:
                pins.append(token)


_MixerWeights = collections.namedtuple(
    "_MixerWeights", "conv_w conv_b ln_g ln_b pool_w pool_scale mu w0 w_up a0 a_up g_up k_k k_a "
                     "r_k gn_g gn_b")


def _mixer_kernel(start_pos, z_ref, hc_ref, hp_ref, hs_ref, s0_ref, ones_ref, *rest):
    n_w = len(_MixerWeights._fields)
    W = _MixerWeights(*rest[:n_w])
    mix_ref, newc_ref, newp_ref, news_ref, news_wkv_ref, ubuf, pbuf, qprev, s_scr = rest[n_w:]
    C = CHUNK
    t = pl.program_id(1)

    @pl.when(t == 0)
    def _():
        zero = jnp.zeros((HEAD_SIZE, HEAD_SIZE), F32)
        for i in range(SEQS_PER_STEP):
            ubuf[i, 0:2, :] = jnp.zeros((2, CONV_DIM), F32)
            ubuf[i, 2:32, :] = hc_ref[i]
            pbuf[i, 0:1, :] = jnp.zeros((1, POOL_DIM), F32)
            pbuf[i, 1:16, :] = hp_ref[i]
            qprev[i] = jnp.zeros(qprev.shape[1:], F32)
            qprev[i, 0:1, :] = hs_ref[i]
            for p in range(PAIRS):
                top = jnp.concatenate([s0_ref[i, 2 * p], zero], axis=-1)
                bot = jnp.concatenate([zero, s0_ref[i, 2 * p + 1]], axis=-1)
                s_scr[i, p] = jnp.concatenate([top, bot], axis=0)

    ones_bd = ones_ref[...]
    pos0 = start_pos + t * C
    seqs = [_Seq(i, z_ref, ubuf, pbuf, qprev, s_scr, mix_ref) for i in range(SEQS_PER_STEP)]
    prep = [_rwkv_prep_pieces(s, W, ones_bd) for s in seqs]
    side = [_conv_pool_pieces(s, W, pos0) for s in seqs]
    post = [_rwkv_post_pieces(s, W, ones_bd) for s in seqs]
    pins = []
    never = t < 0
    rounds = [range(i, i + SEQS_PER_ROUND) for i in range(0, SEQS_PER_STEP, SEQS_PER_ROUND)]
    gather = lambda pieces, ids: [f for i in ids for f in pieces[i]]
    for f in gather(prep, rounds[0]):
        f()
    for r, ids in enumerate(rounds):
        stages = [st for group in zip(*[_wkv_stages(seqs[i], pins, never) for i in ids])
                  for st in group]
        fillers = gather(side, ids)
        if r + 1 < len(rounds):
            fillers += gather(prep, rounds[r + 1])
        if r:
            fillers += gather(post, rounds[r - 1])
        _run_interleaved(stages, fillers, pins)
    for f in gather(post, rounds[-1]):
        f()

    @pl.when(t == pl.num_programs(1) - 1)
    def _():
        for i in range(SEQS_PER_STEP):
            newc_ref[i] = ubuf[i, 2:32, :]
            newp_ref[i] = pbuf[i, 1:16, :]
            news_ref[i] = qprev[i, 0:1, :]
            for p in range(PAIRS):
                sp = s_scr[i, p]
                news_wkv_ref[i, 2 * p] = sp[0:HEAD_SIZE, 0:HEAD_SIZE]
                news_wkv_ref[i, 2 * p + 1] = sp[HEAD_SIZE:, HEAD_SIZE:]


def _mixers(z, hist_conv, hist_pool, hist_shift, state_wkv, ones_bd, wts, layer, start_pos):
    B, L, _ = z.shape
    C, G = CHUNK, SEQS_PER_STEP
    assert L % C == 0 and B % G == 0

    def per_layer(shape):
        nd = len(shape)
        return pl.BlockSpec((None,) + shape, lambda b, t: (layer,) + (0,) * nd)

    def per_seq(shape):
        nd = len(shape)
        return pl.BlockSpec((None, G) + shape, lambda b, t: (layer, b) + (0,) * nd)

    def out_seq(shape):
        nd = len(shape)
        return pl.BlockSpec((G,) + shape, lambda b, t: (b,) + (0,) * nd)

    vec = lambda n: per_layer((1, n))
    in_specs = [
        pl.BlockSpec((G, C, IN_PROJ), lambda b, t: (b, t, 0)),
        per_seq((CONV_HIST, CONV_DIM)),
        per_seq((POOL_HIST, POOL_DIM)),
        per_seq((1, RWKV_PROJ)),
        per_seq((RWKV_HEADS, HEAD_SIZE, HEAD_SIZE)),
        pl.BlockSpec((LANES, LANES), lambda b, t: (0, 0)),
        per_layer((CONV_WIDTH, CONV_DIM)), vec(CONV_DIM), vec(CONV_DIM), vec(CONV_DIM),
        per_layer((len(POOL_WINDOWS), POOL_GROUP_DIM, POOL_GROUP_DIM)), vec(POOL_DIM),
        vec(RWKV_PROJ), vec(RWKV_DIM), per_layer((LORA, RWKV_DIM)), vec(RWKV_DIM),
        per_layer((LORA, RWKV_DIM)), per_layer((LORA, RWKV_DIM)),
        vec(RWKV_DIM), vec(RWKV_DIM), vec(RWKV_DIM), vec(RWKV_DIM), vec(RWKV_DIM),
    ]
    out_specs = [
        pl.BlockSpec((G, C, D_MODEL), lambda b, t: (b, t, 0)),
        out_seq((CONV_HIST, CONV_DIM)),
        out_seq((POOL_HIST, POOL_DIM)),
        out_seq((1, RWKV_PROJ)),
        out_seq((RWKV_HEADS, HEAD_SIZE, HEAD_SIZE)),
    ]
    out_shape = [
        jax.ShapeDtypeStruct((B, L, D_MODEL), BF16),
        jax.ShapeDtypeStruct((B, CONV_HIST, CONV_DIM), F32),
        jax.ShapeDtypeStruct((B, POOL_HIST, POOL_DIM), F32),
        jax.ShapeDtypeStruct((B, 1, RWKV_PROJ), F32),
        jax.ShapeDtypeStruct((B, RWKV_HEADS, HEAD_SIZE, HEAD_SIZE), F32),
    ]
    scratch = [
        pltpu.VMEM((G, 32 + C, CONV_DIM), F32),
        pltpu.VMEM((G, 16 + C, POOL_DIM), F32),
        pltpu.VMEM((G, SUBLANES, RWKV_PROJ), F32),
        pltpu.VMEM((G, PAIRS, LANES, LANES), F32),
    ]
    return pl.pallas_call(
        functools.partial(_mixer_kernel, start_pos),
        grid=(B // G, L // C),
        in_specs=in_specs,
        out_specs=out_specs,
        out_shape=out_shape,
        scratch_shapes=scratch,
        compiler_params=pltpu.CompilerParams(
            dimension_semantics=("arbitrary", "arbitrary"), vmem_limit_bytes=VMEM_LIMIT),
    )(z, hist_conv, hist_pool, hist_shift, state_wkv, ones_bd, *wts)


def _prepare_weights(weights):
    weights = list(weights)
    w_in, w_out, ffn_down = weights[1], weights[19], weights[23]
    weights[1] = _transpose_cast_weight(jnp.swapaxes(w_in, 1, 2), IN_PROJ_PAD, 1024)
    weights[19] = _cast_weight(w_out, 512)
    weights[23] = _cast_weight(ffn_down, 512)
    return tuple(weights)


def _trunk(x, hist_conv, hist_pool, hist_shift, state_wkv, start_pos, weights):
    (norm_mix, w_in, conv_w, conv_b, conv_ln_g, conv_ln_b, pool_w, pool_scale,
     shift_mu, decay_w0, decay_up, iclr_a0, iclr_up, gate_up, k_k, k_a, r_k, gn_g, gn_b,
     w_out, norm_ffn, ffn_gate, ffn_up, ffn_down, norm_final) = weights
    B, L, _ = x.shape
    T = B * L
    as_rows = lambda a: a.reshape(DEPTH, 1, a.shape[-1])
    mixer_wts = (conv_w, as_rows(conv_b), as_rows(conv_ln_g), as_rows(conv_ln_b), pool_w,
                 as_rows(pool_scale), as_rows(shift_mu), as_rows(decay_w0), decay_up,
                 as_rows(iclr_a0), iclr_up, gate_up, as_rows(k_k), as_rows(k_a),
                 as_rows(r_k.reshape(DEPTH, RWKV_DIM)), as_rows(gn_g), as_rows(gn_b))
    norm_mix3 = as_rows(norm_mix)
    idx = jnp.arange(LANES) // HEAD_SIZE
    ones_bd = (idx[:, None] == idx[None, :]).astype(BF16)

    x = x.reshape(T, D_MODEL)
    h = x
    convs, pools, shifts, wkvs = [], [], [], []
    for layer in range(DEPTH):
        last = layer == DEPTH - 1
        z = _in_proj(h, norm_mix3, w_in, layer, fuse_norm=layer == 0)
        mix, c_new, p_new, s_new, S_new = _mixers(
            z.reshape(B, L, IN_PROJ), hist_conv, hist_pool, hist_shift, state_wkv, ones_bd,
            mixer_wts, layer, start_pos)
        x, h = _residual_proj(x, mix.reshape(T, D_MODEL), w_out, layer, norm_ffn[layer][None],
                              tk=512, final=False)
        act = _ffn_up(h, ffn_gate, ffn_up, layer)
        gain = norm_final[None] if last else norm_mix[layer + 1][None]
        out = _residual_proj(x, act, ffn_down, layer, gain, tk=512, final=last)
        if not last:
            x, h = out
        convs.append(c_new)
        pools.append(p_new)
        shifts.append(s_new)
        wkvs.append(S_new)
    y = out.reshape(B, L, D_MODEL)
    return y, jnp.stack(convs), jnp.stack(pools), jnp.stack(shifts), jnp.stack(wkvs)


def kernel(x_prompt, x_sample, cache_conv, cache_pool, state_shift, state_wkv, norm_mix, w_in,
           conv_w, conv_b, conv_ln_g, conv_ln_b, pool_w, pool_scale, shift_mu, decay_w0, decay_up,
           iclr_a0, iclr_up, gate_up, k_k, k_a, r_k, gn_g, gn_b, w_out, norm_ffn, ffn_gate,
           ffn_up, ffn_down, norm_final):
    weights = (norm_mix, w_in, conv_w, conv_b, conv_ln_g, conv_ln_b, pool_w, pool_scale,
               shift_mu, decay_w0, decay_up, iclr_a0, iclr_up, gate_up, k_k, k_a, r_k, gn_g, gn_b,
               w_out, norm_ffn, ffn_gate, ffn_up, ffn_down, norm_final)
    weights = _prepare_weights(weights)
    bp = x_prompt.shape[0]
    zc = jnp.zeros((DEPTH, bp, CONV_HIST, CONV_DIM), F32)
    zp = jnp.zeros((DEPTH, bp, POOL_HIST, POOL_DIM), F32)
    zs = jnp.zeros((DEPTH, bp, 1, RWKV_PROJ), F32)
    zw = jnp.zeros((DEPTH, bp, RWKV_HEADS, HEAD_SIZE, HEAD_SIZE), F32)
    y_p, p_conv, p_pool, p_shift, p_wkv = _trunk(x_prompt, zc, zp, zs, zw, 0, weights)
    y_s, s_conv, s_pool, s_shift, s_wkv = _trunk(x_sample, cache_conv, cache_pool, state_shift,
                                                 state_wkv, PAST_LEN, weights)
    return (y_p, y_s, p_conv, p_pool, p_shift, p_wkv, s_conv, s_pool, s_shift, s_wkv)
```

```python
import collections
import functools
import math

import jax
import jax.numpy as jnp
from jax import lax
from jax.experimental import pallas as pl
from jax.experimental.pallas import tpu as pltpu

F32 = jnp.float32
BF16 = jnp.bfloat16

D_MODEL = 2048
DEPTH = 4
PAST_LEN = 4096
CONV_DIM = 512
CONV_WIDTH = 31
CONV_HIST = CONV_WIDTH - 1
POOL_DIM = 512
POOL_WINDOWS = (2, 4, 8, 16)
POOL_GROUP_DIM = 128
POOL_HIST = max(POOL_WINDOWS) - 1
RWKV_DIM = 1024
HEAD_SIZE = 64
RWKV_HEADS = RWKV_DIM // HEAD_SIZE
LORA = 64
RWKV_PROJ = 3 * RWKV_DIM + 3 * LORA
IN_PROJ = 2 * CONV_DIM + POOL_DIM + RWKV_PROJ
POOL_OFF = 2 * CONV_DIM
RWKV_OFF = POOL_OFF + POOL_DIM
D_FF = 5632
RMS_EPS = 1e-6
DECAY_SCALE = math.exp(-0.5)
LN_EPS = 1e-5
GN_EPS = 64e-5

LANES = 128
SUBLANES = 8
PAIRS = RWKV_DIM // LANES
SEQS_PER_STEP = 4
SEQS_PER_ROUND = 2
CHUNK = 64
SOLVE_SQUARINGS = 5
VMEM_LIMIT = 56 * 1024 * 1024


def _tiles(n_tokens):
    tm = min(n_tokens, 1024)
    assert n_tokens % tm == 0
    return dict(tm=tm, tn=1024, tn_ff=512, tm_down=min(n_tokens, 512), tk_down=1408)


IN_PROJ_PAD = -(-IN_PROJ // 1024) * 1024


def _cast_kernel(w_ref, o_ref):
    o_ref[...] = w_ref[...].astype(BF16)


def _cast_weight(w, rows):
    depth, K, N = w.shape
    return pl.pallas_call(
        _cast_kernel,
        grid=(depth, K // rows),
        in_specs=[pl.BlockSpec((None, rows, N), lambda l, k: (l, k, 0))],
        out_specs=pl.BlockSpec((None, rows, N), lambda l, k: (l, k, 0)),
        out_shape=jax.ShapeDtypeStruct(w.shape, BF16),
        compiler_params=pltpu.CompilerParams(dimension_semantics=("arbitrary", "arbitrary")),
    )(w)


def _transpose_cast_kernel(n_valid, w_ref, o_ref):
    rows = w_ref.shape[0]
    row = pl.program_id(1) * rows + lax.broadcasted_iota(jnp.int32, w_ref.shape, 0)
    o_ref[...] = jnp.where(row < n_valid, w_ref[...], 0.0).T.astype(BF16)


def _transpose_cast_weight(w_t, n_pad, cols):
    depth, N, K = w_t.shape
    return pl.pallas_call(
        functools.partial(_transpose_cast_kernel, N),
        grid=(depth, n_pad // cols),
        in_specs=[pl.BlockSpec((None, cols, K), lambda l, n: (l, n, 0))],
        out_specs=pl.BlockSpec((None, K, cols), lambda l, n: (l, 0, n)),
        out_shape=jax.ShapeDtypeStruct((depth, K, n_pad), BF16),
        compiler_params=pltpu.CompilerParams(dimension_semantics=("arbitrary", "arbitrary")),
    )(w_t)


def _bdot(a, b):
    return jnp.dot(a.astype(BF16), b.astype(BF16), preferred_element_type=F32)


def _bdot_nt(a, b):
    return lax.dot_general(a.astype(BF16), b.astype(BF16), (((1,), (1,)), ((), ())),
                           preferred_element_type=F32)


def _sigmoid(x):
    return 1.0 / (1.0 + jnp.exp(-x))


def _rms_norm(x, gain):
    ms = jnp.mean(x * x, axis=-1, keepdims=True)
    return x * lax.rsqrt(ms + RMS_EPS) * gain


def _in_proj_kernel(fuse_norm, x_ref, g_ref, w_ref, o_ref, *scratch):
    if fuse_norm:
        h_ref, = scratch

        @pl.when(pl.program_id(1) == 0)
        def _():
            h_ref[...] = _rms_norm(x_ref[...], g_ref[...]).astype(BF16)

        h = h_ref[...]
    else:
        h = x_ref[...]
    o_ref[...] = jnp.dot(h, w_ref[...], preferred_element_type=F32)


def _in_proj(x, g, w, layer, fuse_norm):
    T = x.shape[0]
    t = _tiles(T)
    tm, tn = t["tm"], t["tn"]
    return pl.pallas_call(
        functools.partial(_in_proj_kernel, fuse_norm),
        grid=(T // tm, w.shape[-1] // tn),
        in_specs=[
            pl.BlockSpec((tm, D_MODEL), lambda m, n: (m, 0)),
            pl.BlockSpec((None, 1, D_MODEL), lambda m, n: (layer, 0, 0)),
            pl.BlockSpec((None, D_MODEL, tn), lambda m, n: (layer, 0, n)),
        ],
        out_specs=pl.BlockSpec((tm, tn), lambda m, n: (m, n)),
        out_shape=jax.ShapeDtypeStruct((T, IN_PROJ), F32),
        scratch_shapes=[pltpu.VMEM((tm, D_MODEL), BF16)] if fuse_norm else [],
        compiler_params=pltpu.CompilerParams(
            dimension_semantics=("arbitrary", "arbitrary"), vmem_limit_bytes=VMEM_LIMIT),
    )(x, g, w)


def _out_kernel(x_ref, a_ref, w_ref, o_ref):
    o_ref[...] = x_ref[...] + jnp.dot(a_ref[...], w_ref[...], preferred_element_type=F32)


def _out_proj(x, a, w, layer):
    T = x.shape[0]
    t = _tiles(T)
    tm, tn = t["tm"], t["tn"]
    K = a.shape[-1]
    return pl.pallas_call(
        _out_kernel,
        grid=(T // tm, D_MODEL // tn),
        in_specs=[
            pl.BlockSpec((tm, tn), lambda m, n: (m, n)),
            pl.BlockSpec((tm, K), lambda m, n: (m, 0)),
            pl.BlockSpec((None, K, tn), lambda m, n: (layer, 0, n)),
        ],
        out_specs=pl.BlockSpec((tm, tn), lambda m, n: (m, n)),
        out_shape=jax.ShapeDtypeStruct((T, D_MODEL), F32),
        compiler_params=pltpu.CompilerParams(
            dimension_semantics=("arbitrary", "arbitrary"), vmem_limit_bytes=VMEM_LIMIT),
    )(x, a, w)


def _ffn_up_kernel(x_ref, g_ref, wg_ref, wu_ref, o_ref, h_ref):
    @pl.when(pl.program_id(1) == 0)
    def _():
        h_ref[...] = _rms_norm(x_ref[...], g_ref[...]).astype(BF16)

    h = h_ref[...]
    gate = jnp.dot(h, wg_ref[...].astype(BF16), preferred_element_type=F32)
    up = jnp.dot(h, wu_ref[...].astype(BF16), preferred_element_type=F32)
    o_ref[...] = (gate * _sigmoid(gate) * up).astype(BF16)


def _ffn_up(x, g, wg, wu, layer):
    T = x.shape[0]
    t = _tiles(T)
    tm, tn = t["tm"], t["tn_ff"]
    return pl.pallas_call(
        _ffn_up_kernel,
        grid=(T // tm, D_FF // tn),
        in_specs=[
            pl.BlockSpec((tm, D_MODEL), lambda m, n: (m, 0)),
            pl.BlockSpec((None, 1, D_MODEL), lambda m, n: (layer, 0, 0)),
            pl.BlockSpec((None, D_MODEL, tn), lambda m, n: (layer, 0, n)),
            pl.BlockSpec((None, D_MODEL, tn), lambda m, n: (layer, 0, n)),
        ],
        out_specs=pl.BlockSpec((tm, tn), lambda m, n: (m, n)),
        out_shape=jax.ShapeDtypeStruct((T, D_FF), BF16),
        scratch_shapes=[pltpu.VMEM((tm, D_MODEL), BF16)],
        compiler_params=pltpu.CompilerParams(
            dimension_semantics=("arbitrary", "arbitrary"), vmem_limit_bytes=VMEM_LIMIT),
    )(x, g, wg, wu)


def _ffn_down_kernel(final, x_ref, a_ref, w_ref, g_ref, o_ref, *h_ref):
    k = pl.program_id(1)

    @pl.when(k == 0)
    def _():
        o_ref[...] = x_ref[...]

    o_ref[...] += jnp.dot(a_ref[...], w_ref[...], preferred_element_type=F32)

    @pl.when(k == pl.num_programs(1) - 1)
    def _():
        normed = _rms_norm(o_ref[...], g_ref[...])
        if final:
            o_ref[...] = normed
        else:
            h_ref[0][...] = normed.astype(BF16)


def _ffn_down(x, a, w, layer, gain, final):
    T, K = a.shape
    t = _tiles(T)
    tm, tk = t["tm_down"], t["tk_down"]
    row_spec = pl.BlockSpec((tm, D_MODEL), lambda m, k: (m, 0))
    out_shape = [jax.ShapeDtypeStruct((T, D_MODEL), F32)]
    if not final:
        out_shape.append(jax.ShapeDtypeStruct((T, D_MODEL), BF16))
    out = pl.pallas_call(
        functools.partial(_ffn_down_kernel, final),
        grid=(T // tm, K // tk),
        in_specs=[
            row_spec,
            pl.BlockSpec((tm, tk), lambda m, k: (m, k)),
            pl.BlockSpec((None, tk, D_MODEL), lambda m, k: (layer, k, 0)),
            pl.BlockSpec((1, D_MODEL), lambda m, k: (0, 0)),
        ],
        out_specs=[row_spec] * len(out_shape),
        out_shape=out_shape,
        compiler_params=pltpu.CompilerParams(
            dimension_semantics=("arbitrary", "arbitrary"), vmem_limit_bytes=VMEM_LIMIT),
    )(x, a, w, gain)
    return out[0] if final else out


class _Seq:
    def __init__(self, i, z_ref, ubuf, pbuf, qprev, s_scr, mix_ref):
        self.z = z_ref.at[i]
        self.ubuf = ubuf.at[i]
        self.pbuf = pbuf.at[i]
        self.qprev = qprev.at[i]
        self.s = s_scr.at[i]
        self.mix = mix_ref.at[i]
        self.v = {}


def _head_sums(x, ones_bd):
    stacked = jnp.concatenate([x[:, p * LANES:(p + 1) * LANES] for p in range(PAIRS)], axis=0)
    s = _bdot(stacked, ones_bd)
    C = x.shape[0]
    return jnp.concatenate([s[p * C:(p + 1) * C, :] for p in range(PAIRS)], axis=-1)


def _lane_tile_sum(x):
    tiles = [x[:, i * LANES:(i + 1) * LANES] for i in range(x.shape[-1] // LANES)]
    while len(tiles) > 1:
        tiles = [a + b for a, b in zip(tiles[0::2], tiles[1::2])] + tiles[len(tiles) & ~1:]
    return tiles[0]


def _conv_pool_pieces(s, W, pos0):
    C = CHUNK
    v = s.v

    def glu():
        val = s.z[:, 0:CONV_DIM]
        gate = s.z[:, CONV_DIM:2 * CONV_DIM]
        s.ubuf[32:32 + C, :] = val * _sigmoid(gate)

    def taps(tile):
        def f():
            lanes = slice(tile * LANES, (tile + 1) * LANES)
            rows = 32 + C
            full = s.ubuf[:, lanes]
            shifted = [full] + [pltpu.roll(full, rows - k, 0) for k in range(1, SUBLANES)]
            acc = jnp.zeros((C, LANES), F32) + W.conv_b[:, lanes]
            for j in range(CONV_WIDTH):
                phase, base = (2 + j) % SUBLANES, (2 + j) // SUBLANES * SUBLANES
                acc = acc + shifted[phase][base:base + C, :] * W.conv_w[j:j + 1, lanes]
            v["conv%d" % tile] = acc
            return acc
        return f

    def norm():
        acc = jnp.concatenate([v.pop("conv%d" % t) for t in range(CONV_DIM // LANES)], axis=-1)
        mu = jnp.mean(acc, axis=-1, keepdims=True)
        cen = acc - mu
        var = jnp.mean(cen * cen, axis=-1, keepdims=True)
        hn = cen * lax.rsqrt(var + LN_EPS) * W.ln_g[...] + W.ln_b[...]
        out = hn * _sigmoid(hn)
        s.mix[:, 0:CONV_DIM] = out.astype(BF16)
        s.ubuf[2:32, :] = s.ubuf[C + 2:C + 32, :]
        return _lane_tile_sum(out)

    def pool():
        s.pbuf[16:16 + C, :] = s.z[:, POOL_OFF:POOL_OFF + POOL_DIM]
        pos = pos0 + lax.broadcasted_iota(jnp.int32, (C, POOL_GROUP_DIM), 0)
        outs = []
        for gi, w in enumerate(POOL_WINDOWS):
            lo, hi = gi * POOL_GROUP_DIM, (gi + 1) * POOL_GROUP_DIM
            tok = s.pbuf[16:16 + C, lo:hi]
            tot = tok
            for j in range(1, w):
                tot = tot + s.pbuf[16 - j:16 - j + C, lo:hi]
            cnt = jnp.minimum(w, pos + 1).astype(F32)
            d = tot / cnt - tok
            outs.append(_bdot(d, W.pool_w[gi]))
        out = jnp.concatenate(outs, axis=-1) * W.pool_scale[...]
        s.mix[:, CONV_DIM:CONV_DIM + POOL_DIM] = out.astype(BF16)
        s.pbuf[1:16, :] = s.pbuf[C + 1:C + 16, :]
        return _lane_tile_sum(out)

    return [glu] + [taps(t) for t in range(CONV_DIM // LANES)] + [norm, pool]


def _rwkv_prep_pieces(s, W, ones_bd):
    C = CHUNK
    v = s.v

    def shift(name, off, width):
        def f():
            q = s.z[:, RWKV_OFF + off:RWKV_OFF + off + width]
            rolled = pltpu.roll(q, 1, 0)
            first_row = lax.broadcasted_iota(jnp.int32, (C, width), 0) == 0
            q_prev = jnp.where(first_row, s.qprev[0:1, off:off + width], rolled)
            v[name] = q + (q_prev - q) * W.mu[:, off:off + width]
            s.qprev[0:1, off:off + width] = q[C - 1:C, :]
        return f

    def lora():
        lo = v.pop("lo")
        w_lo, a_lo, g_lo = lo[:, 0:LORA], lo[:, LORA:2 * LORA], lo[:, 2 * LORA:3 * LORA]
        x = W.w0[...] + _bdot(jnp.tanh(w_lo), W.w_up[...])
        v["lw"] = -DECAY_SCALE * _sigmoid(x)
        v["a"] = _sigmoid(W.a0[...] + _bdot(a_lo, W.a_up[...]))
        v["g"] = _bdot(_sigmoid(g_lo), W.g_up[...])

    def keys():
        k, a = v.pop("k"), v["a"]
        kk = k * W.k_k[...]
        kk = kk * lax.rsqrt(jnp.maximum(_head_sums(kk * kk, ones_bd), 1e-24))
        v["kk"] = kk
        v["k2"] = k * (1.0 + (a - 1.0) * W.k_a[...])

    def decay():
        lw = v.pop("lw")
        row = lax.broadcasted_iota(jnp.int32, (C, C), 0)
        col = lax.broadcasted_iota(jnp.int32, (C, C), 1)
        tri = (col <= row).astype(BF16)
        lw_hi = lw.astype(BF16)
        rem = lw - lw_hi.astype(F32)
        lw_mid = rem.astype(BF16)
        lw_lo = (rem - lw_mid.astype(F32)).astype(BF16)
        cum = (jnp.dot(tri, lw_hi, preferred_element_type=F32)
               + jnp.dot(tri, lw_mid, preferred_element_type=F32)
               + jnp.dot(tri, lw_lo, preferred_element_type=F32))
        tot = cum[C - 1:C, :]
        v["e_ex"] = jnp.exp(cum - lw)
        v["e_in"] = jnp.exp(cum)
        v["e_inv"] = jnp.exp(-cum)
        v["e_end"] = jnp.exp(tot - cum)
        v["g_tot"] = jnp.exp(tot)

    def products():
        kk, k2, a = v.pop("kk"), v["k2"], v.pop("a")
        b = kk * a
        e_inv, e_end = v.pop("e_inv"), v.pop("e_end")
        v["A_t"] = -kk * v.pop("e_ex")
        v["R_t"] = v["r"] * v.pop("e_in")
        v["B_t"] = b * e_inv
        v["K_t"] = k2 * e_inv
        v["B_e"] = b * e_end
        v["K_e"] = k2 * e_end

    return [shift("lo", 3 * RWKV_DIM, 3 * LORA), lora, shift("k", RWKV_DIM, RWKV_DIM), keys, decay,
            shift("r", 0, RWKV_DIM), shift("v", 2 * RWKV_DIM, RWKV_DIM), products]


def _wkv_stages(s, pins, never):
    C = CHUNK
    v = s.v
    pairs = range(PAIRS)
    sl = [slice(p * LANES, (p + 1) * LANES) for p in pairs]
    low2 = lax.broadcasted_iota(jnp.int32, (2 * C, LANES), 1) < HEAD_SIZE
    low1 = lax.broadcasted_iota(jnp.int32, (C, LANES), 1) < HEAD_SIZE
    r4 = lax.broadcasted_iota(jnp.int32, (C, 4 * C), 0)
    c4 = lax.broadcasted_iota(jnp.int32, (C, 4 * C), 1) & (C - 1)
    strict = c4 < r4
    incl = c4 <= r4
    rr = lax.broadcasted_iota(jnp.int32, (LANES, LANES), 0)
    cc = lax.broadcasted_iota(jnp.int32, (LANES, LANES), 1)
    same_head = (rr < HEAD_SIZE) == (cc < HEAD_SIZE)
    z_tile = jnp.zeros((C, LANES), BF16)
    z_rows = jnp.zeros((C, 2 * LANES), BF16)
    w = {}

    def masked_rows(p):
        u = w["U"][p]
        return jnp.concatenate([jnp.where(low1, u, 0.0).astype(BF16), w["v_low"][p],
                                jnp.where(low1, 0.0, u).astype(BF16), w["v_high"][p]], axis=0)

    def gram():
        w["S0"] = [s.s[p] for p in pairs]
        w["v_low"] = [jnp.where(low1, v["v"][:, sl[p]], 0.0).astype(BF16) for p in pairs]
        w["v_high"] = [jnp.where(low1, 0.0, v["v"][:, sl[p]]).astype(BF16) for p in pairs]
        A_t, R_t, B_t, K_t = v.pop("A_t"), v.pop("R_t"), v.pop("B_t"), v.pop("K_t")
        GG = []
        for p in pairs:
            AR = jnp.concatenate([A_t[:, sl[p]], R_t[:, sl[p]]], axis=0)
            lhs = jnp.concatenate([jnp.where(low2, AR, 0.0), jnp.where(low2, 0.0, AR)], axis=0)
            rhs = jnp.concatenate([B_t[:, sl[p]], K_t[:, sl[p]], w["S0"][p]], axis=0)
            GG.append(_bdot_nt(lhs, rhs))
        side = lambda g, r0: jnp.concatenate(
            [g[r0:r0 + C, 0:2 * C], g[r0 + 2 * C:r0 + 3 * C, 0:2 * C]], axis=-1)
        w["P"] = [jnp.where(strict, side(g, 0), 0.0).astype(BF16) for g in GG]
        w["M"] = [jnp.where(incl, side(g, C), 0.0).astype(BF16) for g in GG]
        w["U"] = [g[0:C, 2 * C:] + g[2 * C:3 * C, 2 * C:] for g in GG]
        w["AS_bot"] = [g[C:2 * C, 2 * C:] + g[3 * C:4 * C, 2 * C:] for g in GG]

    def take_pins():
        while pins:
            w["U"][PAIRS - 1] = jnp.where(never, pins.pop(), w["U"][PAIRS - 1])

    def level(square):
        def f():
            take_pins()
            Z = [masked_rows(p) for p in pairs]
            P = w["P"]
            w["U"] = [w["U"][p] + jnp.dot(P[p], Z[p], preferred_element_type=F32) for p in pairs]
            if square:
                sq = [jnp.concatenate([
                    jnp.concatenate([P[p][:, 0:2 * C], z_tile], axis=-1), z_rows,
                    jnp.concatenate([z_tile, P[p][:, 2 * C:]], axis=-1), z_rows], axis=0)
                    for p in pairs]
                w["P"] = [jnp.dot(P[p], sq[p], preferred_element_type=F32).astype(BF16)
                          for p in pairs]
        return f

    def finish():
        take_pins()
        B_e, K_e, g_tot = v.pop("B_e"), v.pop("K_e"), v.pop("g_tot")
        ys = []
        for p in pairs:
            ys.append(w["AS_bot"][p]
                      + jnp.dot(w["M"][p], masked_rows(p), preferred_element_type=F32))
            UV = jnp.concatenate([w["U"][p], v["v"][:, sl[p]]], axis=0)
            BKe = jnp.concatenate([B_e[:, sl[p]], K_e[:, sl[p]]], axis=0)
            upd = _bdot(UV.T, BKe)
            s.s[p] = w["S0"][p] * g_tot[:, sl[p]] + jnp.where(same_head, upd, 0.0)
        v["y"] = jnp.concatenate(ys, axis=-1)
        w.clear()

    return [gram] + [level(True) for _ in range(SOLVE_SQUARINGS)] + [level(False), finish]


def _rwkv_post_pieces(s, W, ones_bd):
    v = s.v
    inv_n = 1.0 / HEAD_SIZE

    def center():
        y = v.pop("y")
        v["yc"] = y - _head_sums(y, ones_bd) * inv_n

    def scale():
        yc = v.pop("yc")
        yv = _head_sums(yc * yc, ones_bd) * inv_n
        v["yn"] = yc * lax.rsqrt(yv + GN_EPS) * W.gn_g[...] + W.gn_b[...]

    def bonus():
        v["bonus"] = _head_sums(v.pop("r") * v.pop("k2") * W.r_k[...], ones_bd) * v.pop("v")

    def gate():
        out = (v.pop("yn") + v.pop("bonus")) * v.pop("g")
        s.mix[:, CONV_DIM + POOL_DIM:] = out.astype(BF16)
        return _lane_tile_sum(out)

    return [center, scale, bonus, gate]


def _run_interleaved(stages, fillers, pins):
    n = len(stages)
    for i, stage in enumerate(stages):
        stage()
        for f in fillers[i * len(fillers) // n:(i + 1) * len(fillers) // n]:
            token = f()
            if token is not None:
                pins.append(token)


_MixerWeights = collections.namedtuple(
    "_MixerWeights", "conv_w conv_b ln_g ln_b pool_w pool_scale mu w0 w_up a0 a_up g_up k_k k_a "
                     "r_k gn_g gn_b")


def _mixer_kernel(start_pos, z_ref, hc_ref, hp_ref, hs_ref, s0_ref, ones_ref, *rest):
    n_w = len(_MixerWeights._fields)
    W = _MixerWeights(*rest[:n_w])
    mix_ref, newc_ref, newp_ref, news_ref, news_wkv_ref, ubuf, pbuf, qprev, s_scr = rest[n_w:]
    C = CHUNK
    t = pl.program_id(1)

    @pl.when(t == 0)
    def _():
        zero = jnp.zeros((HEAD_SIZE, HEAD_SIZE), F32)
        for i in range(SEQS_PER_STEP):
            ubuf[i, 0:2, :] = jnp.zeros((2, CONV_DIM), F32)
            ubuf[i, 2:32, :] = hc_ref[i]
            pbuf[i, 0:1, :] = jnp.zeros((1, POOL_DIM), F32)
            pbuf[i, 1:16, :] = hp_ref[i]
            qprev[i] = jnp.zeros(qprev.shape[1:], F32)
            qprev[i, 0:1, :] = hs_ref[i]
            for p in range(PAIRS):
                top = jnp.concatenate([s0_ref[i, 2 * p], zero], axis=-1)
                bot = jnp.concatenate([zero, s0_ref[i, 2 * p + 1]], axis=-1)
                s_scr[i, p] = jnp.concatenate([top, bot], axis=0)

    ones_bd = ones_ref[...]
    pos0 = start_pos + t * C
    seqs = [_Seq(i, z_ref, ubuf, pbuf, qprev, s_scr, mix_ref) for i in range(SEQS_PER_STEP)]
    prep = [_rwkv_prep_pieces(s, W, ones_bd) for s in seqs]
    side = [_conv_pool_pieces(s, W, pos0) for s in seqs]
    post = [_rwkv_post_pieces(s, W, ones_bd) for s in seqs]
    pins = []
    never = t < 0
    rounds = [range(i, i + SEQS_PER_ROUND) for i in range(0, SEQS_PER_STEP, SEQS_PER_ROUND)]
    gather = lambda pieces, ids: [f for i in ids for f in pieces[i]]
    for f in gather(prep, rounds[0]):
        f()
    for r, ids in enumerate(rounds):
        stages = [st for group in zip(*[_wkv_stages(seqs[i], pins, never) for i in ids])
                  for st in group]
        fillers = gather(side, ids)
        if r + 1 < len(rounds):
            fillers += gather(prep, rounds[r + 1])
        if r:
            fillers += gather(post, rounds[r - 1])
        _run_interleaved(stages, fillers, pins)
    for f in gather(post, rounds[-1]):
        f()

    @pl.when(t == pl.num_programs(1) - 1)
    def _():
        for i in range(SEQS_PER_STEP):
            newc_ref[i] = ubuf[i, 2:32, :]
            newp_ref[i] = pbuf[i, 1:16, :]
            news_ref[i] = qprev[i, 0:1, :]
            for p in range(PAIRS):
                sp = s_scr[i, p]
                news_wkv_ref[i, 2 * p] = sp[0:HEAD_SIZE, 0:HEAD_SIZE]
                news_wkv_ref[i, 2 * p + 1] = sp[HEAD_SIZE:, HEAD_SIZE:]


def _mixers(z, hist_conv, hist_pool, hist_shift, state_wkv, ones_bd, wts, layer, start_pos):
    B, L, _ = z.shape
    C, G = CHUNK, SEQS_PER_STEP
    assert L % C == 0 and B % G == 0

    def per_layer(shape):
        nd = len(shape)
        return pl.BlockSpec((None,) + shape, lambda b, t: (layer,) + (0,) * nd)

    def per_seq(shape):
        nd = len(shape)
        return pl.BlockSpec((None, G) + shape, lambda b, t: (layer, b) + (0,) * nd)

    def out_seq(shape):
        nd = len(shape)
        return pl.BlockSpec((G,) + shape, lambda b, t: (b,) + (0,) * nd)

    vec = lambda n: per_layer((1, n))
    in_specs = [
        pl.BlockSpec((G, C, IN_PROJ), lambda b, t: (b, t, 0)),
        per_seq((CONV_HIST, CONV_DIM)),
        per_seq((POOL_HIST, POOL_DIM)),
        per_seq((1, RWKV_PROJ)),
        per_seq((RWKV_HEADS, HEAD_SIZE, HEAD_SIZE)),
        pl.BlockSpec((LANES, LANES), lambda b, t: (0, 0)),
        per_layer((CONV_WIDTH, CONV_DIM)), vec(CONV_DIM), vec(CONV_DIM), vec(CONV_DIM),
        per_layer((len(POOL_WINDOWS), POOL_GROUP_DIM, POOL_GROUP_DIM)), vec(POOL_DIM),
        vec(RWKV_PROJ), vec(RWKV_DIM), per_layer((LORA, RWKV_DIM)), vec(RWKV_DIM),
        per_layer((LORA, RWKV_DIM)), per_layer((LORA, RWKV_DIM)),
        vec(RWKV_DIM), vec(RWKV_DIM), vec(RWKV_DIM), vec(RWKV_DIM), vec(RWKV_DIM),
    ]
    out_specs = [
        pl.BlockSpec((G, C, D_MODEL), lambda b, t: (b, t, 0)),
        out_seq((CONV_HIST, CONV_DIM)),
        out_seq((POOL_HIST, POOL_DIM)),
        out_seq((1, RWKV_PROJ)),
        out_seq((RWKV_HEADS, HEAD_SIZE, HEAD_SIZE)),
    ]
    out_shape = [
        jax.ShapeDtypeStruct((B, L, D_MODEL), BF16),
        jax.ShapeDtypeStruct((B, CONV_HIST, CONV_DIM), F32),
        jax.ShapeDtypeStruct((B, POOL_HIST, POOL_DIM), F32),
        jax.ShapeDtypeStruct((B, 1, RWKV_PROJ), F32),
        jax.ShapeDtypeStruct((B, RWKV_HEADS, HEAD_SIZE, HEAD_SIZE), F32),
    ]
    scratch = [
        pltpu.VMEM((G, 32 + C, CONV_DIM), F32),
        pltpu.VMEM((G, 16 + C, POOL_DIM), F32),
        pltpu.VMEM((G, SUBLANES, RWKV_PROJ), F32),
        pltpu.VMEM((G, PAIRS, LANES, LANES), F32),
    ]
    return pl.pallas_call(
        functools.partial(_mixer_kernel, start_pos),
        grid=(B // G, L // C),
        in_specs=in_specs,
        out_specs=out_specs,
        out_shape=out_shape,
        scratch_shapes=scratch,
        compiler_params=pltpu.CompilerParams(
            dimension_semantics=("arbitrary", "arbitrary"), vmem_limit_bytes=VMEM_LIMIT),
    )(z, hist_conv, hist_pool, hist_shift, state_wkv, ones_bd, *wts)


def _prepare_weights(weights):
    weights = list(weights)
    w_in, w_out, ffn_down = weights[1], weights[19], weights[23]
    weights[1] = _transpose_cast_weight(jnp.swapaxes(w_in, 1, 2), IN_PROJ_PAD, 1024)
    weights[19] = _cast_weight(w_out, 512)
    weights[23] = _cast_weight(ffn_down, 512)
    return tuple(weights)


def _trunk(x, hist_conv, hist_pool, hist_shift, state_wkv, start_pos, weights):
    (norm_mix, w_in, conv_w, conv_b, conv_ln_g, conv_ln_b, pool_w, pool_scale,
     shift_mu, decay_w0, decay_up, iclr_a0, iclr_up, gate_up, k_k, k_a, r_k, gn_g, gn_b,
     w_out, norm_ffn, ffn_gate, ffn_up, ffn_down, norm_final) = weights
    B, L, _ = x.shape
    T = B * L
    as_rows = lambda a: a.reshape(DEPTH, 1, a.shape[-1])
    mixer_wts = (conv_w, as_rows(conv_b), as_rows(conv_ln_g), as_rows(conv_ln_b), pool_w,
                 as_rows(pool_scale), as_rows(shift_mu), as_rows(decay_w0), decay_up,
                 as_rows(iclr_a0), iclr_up, gate_up, as_rows(k_k), as_rows(k_a),
                 as_rows(r_k.reshape(DEPTH, RWKV_DIM)), as_rows(gn_g), as_rows(gn_b))
    norm_mix3 = as_rows(norm_mix)
    norm_ffn3 = as_rows(norm_ffn)
    idx = jnp.arange(LANES) // HEAD_SIZE
    ones_bd = (idx[:, None] == idx[None, :]).astype(BF16)

    x = x.reshape(T, D_MODEL)
    h = x
    convs, pools, shifts, wkvs = [], [], [], []
    for layer in range(DEPTH):
        last = layer == DEPTH - 1
        z = _in_proj(h, norm_mix3, w_in, layer, fuse_norm=layer == 0)
        mix, c_new, p_new, s_new, S_new = _mixers(
            z.reshape(B, L, IN_PROJ), hist_conv, hist_pool, hist_shift, state_wkv, ones_bd,
            mixer_wts, layer, start_pos)
        x = _out_proj(x, mix.reshape(T, D_MODEL), w_out, layer)
        act = _ffn_up(x, norm_ffn3, ffn_gate, ffn_up, layer)
        gain = norm_final[None] if last else norm_mix[layer + 1][None]
        out = _ffn_down(x, act, ffn_down, layer, gain, final=last)
        if not last:
            x, h = out
        convs.append(c_new)
        pools.append(p_new)
        shifts.append(s_new)
        wkvs.append(S_new)
    y = out.reshape(B, L, D_MODEL)
    return y, jnp.stack(convs), jnp.stack(pools), jnp.stack(shifts), jnp.stack(wkvs)


def kernel(x_prompt, x_sample, cache_conv, cache_pool, state_shift, state_wkv, norm_mix, w_in,
           conv_w, conv_b, conv_ln_g, conv_ln_b, pool_w, pool_scale, shift_mu, decay_w0, decay_up,
           iclr_a0, iclr_up, gate_up, k_k, k_a, r_k, gn_g, gn_b, w_out, norm_ffn, ffn_gate,
           ffn_up, ffn_down, norm_final):
    weights = (norm_mix, w_in, conv_w, conv_b, conv_ln_g, conv_ln_b, pool_w, pool_scale,
               shift_mu, decay_w0, decay_up, iclr_a0, iclr_up, gate_up, k_k, k_a, r_k, gn_g, gn_b,
               w_out, norm_ffn, ffn_gate, ffn_up, ffn_down, norm_final)
    weights = _prepare_weights(weights)
    bp = x_prompt.shape[0]
    zc = jnp.zeros((DEPTH, bp, CONV_HIST, CONV_DIM), F32)
    zp = jnp.zeros((DEPTH, bp, POOL_HIST, POOL_DIM), F32)
    zs = jnp.zeros((DEPTH, bp, 1, RWKV_PROJ), F32)
    zw = jnp.zeros((DEPTH, bp, RWKV_HEADS, HEAD_SIZE, HEAD_SIZE), F32)
    y_p, p_conv, p_pool, p_shift, p_wkv = _trunk(x_prompt, zc, zp, zs, zw, 0, weights)
    y_s, s_conv, s_pool, s_shift, s_wkv = _trunk(x_sample, cache_conv, cache_pool, state_shift,
                                                 state_wkv, PAST_LEN, weights)
    return (y_p, y_s, p_conv, p_pool, p_shift, p_wkv, s_conv, s_pool, s_shift, s_wkv)
```

```python
import collections
import functools
import math

import jax
import jax.numpy as jnp
from jax import lax
from jax.experimental import pallas as pl
from jax.experimental.pallas import tpu as pltpu

F32 = jnp.float32
BF16 = jnp.bfloat16

D_MODEL = 2048
DEPTH = 4
PAST_LEN = 4096
CONV_DIM = 512
CONV_WIDTH = 31
CONV_HIST = CONV_WIDTH - 1
POOL_DIM = 512
POOL_WINDOWS = (2, 4, 8, 16)
POOL_GROUP_DIM = 128
POOL_HIST = max(POOL_WINDOWS) - 1
RWKV_DIM = 1024
HEAD_SIZE = 64
RWKV_HEADS = RWKV_DIM // HEAD_SIZE
LORA = 64
RWKV_PROJ = 3 * RWKV_DIM + 3 * LORA
IN_PROJ = 2 * CONV_DIM + POOL_DIM + RWKV_PROJ
POOL_OFF = 2 * CONV_DIM
RWKV_OFF = POOL_OFF + POOL_DIM
D_FF = 5632
RMS_EPS = 1e-6
DECAY_SCALE = math.exp(-0.5)
LN_EPS = 1e-5
GN_EPS = 64e-5

LANES = 128
SUBLANES = 8
PAIRS = RWKV_DIM // LANES
SEQS_PER_STEP = 4
SEQS_PER_ROUND = 2
CHUNK = 64
SOLVE_SQUARINGS = 5
VMEM_LIMIT = 56 * 1024 * 1024


def _tiles(n_tokens):
    tm = min(n_tokens, 1024)
    assert n_tokens % tm == 0
    return dict(tm=tm, tn=1024, tn_ff=512, tk_res=512)


IN_PROJ_PAD = -(-IN_PROJ // 1024) * 1024


def _cast_kernel(w_ref, o_ref):
    o_ref[...] = w_ref[...].astype(BF16)


def _cast_weight(w, rows):
    depth, K, N = w.shape
    return pl.pallas_call(
        _cast_kernel,
        grid=(depth, K // rows),
        in_specs=[pl.BlockSpec((None, rows, N), lambda l, k: (l, k, 0))],
        out_specs=pl.BlockSpec((None, rows, N), lambda l, k: (l, k, 0)),
        out_shape=jax.ShapeDtypeStruct(w.shape, BF16),
        compiler_params=pltpu.CompilerParams(dimension_semantics=("arbitrary", "arbitrary")),
    )(w)


def _transpose_cast_kernel(n_valid, w_ref, o_ref):
    rows = w_ref.shape[0]
    row = pl.program_id(1) * rows + lax.broadcasted_iota(jnp.int32, w_ref.shape, 0)
    o_ref[...] = jnp.where(row < n_valid, w_ref[...], 0.0).T.astype(BF16)


def _transpose_cast_weight(w_t, n_pad, cols):
    depth, N, K = w_t.shape
    return pl.pallas_call(
        functools.partial(_transpose_cast_kernel, N),
        grid=(depth, n_pad // cols),
        in_specs=[pl.BlockSpec((None, cols, K), lambda l, n: (l, n, 0))],
        out_specs=pl.BlockSpec((None, K, cols), lambda l, n: (l, 0, n)),
        out_shape=jax.ShapeDtypeStruct((depth, K, n_pad), BF16),
        compiler_params=pltpu.CompilerParams(dimension_semantics=("arbitrary", "arbitrary")),
    )(w_t)


def _bdot(a, b):
    return jnp.dot(a.astype(BF16), b.astype(BF16), preferred_element_type=F32)


def _bdot_nt(a, b):
    return lax.dot_general(a.astype(BF16), b.astype(BF16), (((1,), (1,)), ((), ())),
                           preferred_element_type=F32)


def _sigmoid(x):
    return 1.0 / (1.0 + jnp.exp(-x))


def _rms_norm(x, gain):
    ms = jnp.mean(x * x, axis=-1, keepdims=True)
    return x * lax.rsqrt(ms + RMS_EPS) * gain


def _in_proj_kernel(fuse_norm, x_ref, g_ref, w_ref, o_ref, *scratch):
    if fuse_norm:
        h_ref, = scratch

        @pl.when(pl.program_id(1) == 0)
        def _():
            h_ref[...] = _rms_norm(x_ref[...], g_ref[...]).astype(BF16)

        h = h_ref[...]
    else:
        h = x_ref[...]
    o_ref[...] = jnp.dot(h, w_ref[...], preferred_element_type=F32)


def _in_proj(x, g, w, layer, fuse_norm):
    T = x.shape[0]
    t = _tiles(T)
    tm, tn = t["tm"], t["tn"]
    return pl.pallas_call(
        functools.partial(_in_proj_kernel, fuse_norm),
        grid=(T // tm, w.shape[-1] // tn),
        in_specs=[
            pl.BlockSpec((tm, D_MODEL), lambda m, n: (m, 0)),
            pl.BlockSpec((None, 1, D_MODEL), lambda m, n: (layer, 0, 0)),
            pl.BlockSpec((None, D_MODEL, tn), lambda m, n: (layer, 0, n)),
        ],
        out_specs=pl.BlockSpec((tm, tn), lambda m, n: (m, n)),
        out_shape=jax.ShapeDtypeStruct((T, IN_PROJ), F32),
        scratch_shapes=[pltpu.VMEM((tm, D_MODEL), BF16)] if fuse_norm else [],
        compiler_params=pltpu.CompilerParams(
            dimension_semantics=("arbitrary", "arbitrary"), vmem_limit_bytes=VMEM_LIMIT),
    )(x, g, w)


def _ffn_up_kernel(h_ref, wg_ref, wu_ref, o_ref):
    h = h_ref[...]
    gate = jnp.dot(h, wg_ref[...].astype(BF16), preferred_element_type=F32)
    up = jnp.dot(h, wu_ref[...].astype(BF16), preferred_element_type=F32)
    o_ref[...] = (gate * _sigmoid(gate) * up).astype(BF16)


def _ffn_up(h, wg, wu, layer):
    T = h.shape[0]
    t = _tiles(T)
    tm, tn = t["tm"], t["tn_ff"]
    return pl.pallas_call(
        _ffn_up_kernel,
        grid=(T // tm, D_FF // tn),
        in_specs=[
            pl.BlockSpec((tm, D_MODEL), lambda m, n: (m, 0)),
            pl.BlockSpec((None, D_MODEL, tn), lambda m, n: (layer, 0, n)),
            pl.BlockSpec((None, D_MODEL, tn), lambda m, n: (layer, 0, n)),
        ],
        out_specs=pl.BlockSpec((tm, tn), lambda m, n: (m, n)),
        out_shape=jax.ShapeDtypeStruct((T, D_FF), BF16),
        compiler_params=pltpu.CompilerParams(
            dimension_semantics=("arbitrary", "arbitrary"), vmem_limit_bytes=VMEM_LIMIT),
    )(h, wg, wu)


def _residual_proj_kernel(final, x_ref, a_ref, w_ref, g_ref, o_ref, *h_ref):
    k = pl.program_id(1)

    @pl.when(k == 0)
    def _():
        o_ref[...] = x_ref[...]

    o_ref[...] += jnp.dot(a_ref[...], w_ref[...], preferred_element_type=F32)

    @pl.when(k == pl.num_programs(1) - 1)
    def _():
        normed = _rms_norm(o_ref[...], g_ref[...])
        if final:
            o_ref[...] = normed
        else:
            h_ref[0][...] = normed.astype(BF16)


def _residual_proj(x, a, w, layer, gain, final):
    T, K = a.shape
    t = _tiles(T)
    tm, tk = t["tm"], t["tk_res"]
    row_spec = pl.BlockSpec((tm, D_MODEL), lambda m, k: (m, 0))
    out_shape = [jax.ShapeDtypeStruct((T, D_MODEL), F32)]
    if not final:
        out_shape.append(jax.ShapeDtypeStruct((T, D_MODEL), BF16))
    out = pl.pallas_call(
        functools.partial(_residual_proj_kernel, final),
        grid=(T // tm, K // tk),
        in_specs=[
            row_spec,
            pl.BlockSpec((tm, tk), lambda m, k: (m, k)),
            pl.BlockSpec((None, tk, D_MODEL), lambda m, k: (layer, k, 0)),
            pl.BlockSpec((1, D_MODEL), lambda m, k: (0, 0)),
        ],
        out_specs=[row_spec] * len(out_shape),
        out_shape=out_shape,
        compiler_params=pltpu.CompilerParams(
            dimension_semantics=("arbitrary", "arbitrary"), vmem_limit_bytes=VMEM_LIMIT),
    )(x, a, w, gain)
    return out[0] if final else out


class _Seq:
    def __init__(self, i, z_ref, ubuf, pbuf, qprev, s_scr, mix_ref):
        self.z = z_ref.at[i]
        self.ubuf = ubuf.at[i]
        self.pbuf = pbuf.at[i]
        self.qprev = qprev.at[i]
        self.s = s_scr.at[i]
        self.mix = mix_ref.at[i]
        self.v = {}


def _head_sums(x, ones_bd):
    stacked = jnp.concatenate([x[:, p * LANES:(p + 1) * LANES] for p in range(PAIRS)], axis=0)
    s = _bdot(stacked, ones_bd)
    C = x.shape[0]
    return jnp.concatenate([s[p * C:(p + 1) * C, :] for p in range(PAIRS)], axis=-1)


def _lane_tile_sum(x):
    tiles = [x[:, i * LANES:(i + 1) * LANES] for i in range(x.shape[-1] // LANES)]
    while len(tiles) > 1:
        tiles = [a + b for a, b in zip(tiles[0::2], tiles[1::2])] + tiles[len(tiles) & ~1:]
    return tiles[0]


def _conv_pool_pieces(s, W, pos0):
    C = CHUNK
    v = s.v

    def glu():
        val = s.z[:, 0:CONV_DIM]
        gate = s.z[:, CONV_DIM:2 * CONV_DIM]
        s.ubuf[32:32 + C, :] = val * _sigmoid(gate)

    def taps(tile):
        def f():
            lanes = slice(tile * LANES, (tile + 1) * LANES)
            rows = 32 + C
            full = s.ubuf[:, lanes]
            shifted = [full] + [pltpu.roll(full, rows - k, 0) for k in range(1, SUBLANES)]
            acc = jnp.zeros((C, LANES), F32) + W.conv_b[:, lanes]
            for j in range(CONV_WIDTH):
                phase, base = (2 + j) % SUBLANES, (2 + j) // SUBLANES * SUBLANES
                acc = acc + shifted[phase][base:base + C, :] * W.conv_w[j:j + 1, lanes]
            v["conv%d" % tile] = acc
            return acc
        return f

    def norm():
        acc = jnp.concatenate([v.pop("conv%d" % t) for t in range(CONV_DIM // LANES)], axis=-1)
        mu = jnp.mean(acc, axis=-1, keepdims=True)
        cen = acc - mu
        var = jnp.mean(cen * cen, axis=-1, keepdims=True)
        hn = cen * lax.rsqrt(var + LN_EPS) * W.ln_g[...] + W.ln_b[...]
        out = hn * _sigmoid(hn)
        s.mix[:, 0:CONV_DIM] = out.astype(BF16)
        s.ubuf[2:32, :] = s.ubuf[C + 2:C + 32, :]
        return _lane_tile_sum(out)

    def pool():
        s.pbuf[16:16 + C, :] = s.z[:, POOL_OFF:POOL_OFF + POOL_DIM]
        pos = pos0 + lax.broadcasted_iota(jnp.int32, (C, POOL_GROUP_DIM), 0)
        outs = []
        for gi, w in enumerate(POOL_WINDOWS):
            lo, hi = gi * POOL_GROUP_DIM, (gi + 1) * POOL_GROUP_DIM
            tok = s.pbuf[16:16 + C, lo:hi]
            tot = tok
            for j in range(1, w):
                tot = tot + s.pbuf[16 - j:16 - j + C, lo:hi]
            cnt = jnp.minimum(w, pos + 1).astype(F32)
            d = tot / cnt - tok
            outs.append(_bdot(d, W.pool_w[gi]))
        out = jnp.concatenate(outs, axis=-1) * W.pool_scale[...]
        s.mix[:, CONV_DIM:CONV_DIM + POOL_DIM] = out.astype(BF16)
        s.pbuf[1:16, :] = s.pbuf[C + 1:C + 16, :]
        return _lane_tile_sum(out)

    return [glu] + [taps(t) for t in range(CONV_DIM // LANES)] + [norm, pool]


def _rwkv_prep_pieces(s, W, ones_bd):
    C = CHUNK
    v = s.v

    def shift(name, off, width):
        def f():
            q = s.z[:, RWKV_OFF + off:RWKV_OFF + off + width]
            rolled = pltpu.roll(q, 1, 0)
            first_row = lax.broadcasted_iota(jnp.int32, (C, width), 0) == 0
            q_prev = jnp.where(first_row, s.qprev[0:1, off:off + width], rolled)
            v[name] = q + (q_prev - q) * W.mu[:, off:off + width]
            s.qprev[0:1, off:off + width] = q[C - 1:C, :]
        return f

    def lora():
        lo = v.pop("lo")
        w_lo, a_lo, g_lo = lo[:, 0:LORA], lo[:, LORA:2 * LORA], lo[:, 2 * LORA:3 * LORA]
        x = W.w0[...] + _bdot(jnp.tanh(w_lo), W.w_up[...])
        v["lw"] = -DECAY_SCALE * _sigmoid(x)
        v["a"] = _sigmoid(W.a0[...] + _bdot(a_lo, W.a_up[...]))
        v["g"] = _bdot(_sigmoid(g_lo), W.g_up[...])

    def keys():
        k, a = v.pop("k"), v["a"]
        kk = k * W.k_k[...]
        kk = kk * lax.rsqrt(jnp.maximum(_head_sums(kk * kk, ones_bd), 1e-24))
        v["kk"] = kk
        v["k2"] = k * (1.0 + (a - 1.0) * W.k_a[...])

    def decay():
        lw = v.pop("lw")
        row = lax.broadcasted_iota(jnp.int32, (C, C), 0)
        col = lax.broadcasted_iota(jnp.int32, (C, C), 1)
        tri = (col <= row).astype(BF16)
        lw_hi = lw.astype(BF16)
        rem = lw - lw_hi.astype(F32)
        lw_mid = rem.astype(BF16)
        lw_lo = (rem - lw_mid.astype(F32)).astype(BF16)
        cum = (jnp.dot(tri, lw_hi, preferred_element_type=F32)
               + jnp.dot(tri, lw_mid, preferred_element_type=F32)
               + jnp.dot(tri, lw_lo, preferred_element_type=F32))
        tot = cum[C - 1:C, :]
        v["e_ex"] = jnp.exp(cum - lw)
        v["e_in"] = jnp.exp(cum)
        v["e_inv"] = jnp.exp(-cum)
        v["e_end"] = jnp.exp(tot - cum)
        v["g_tot"] = jnp.exp(tot)

    def products():
        kk, k2, a = v.pop("kk"), v["k2"], v.pop("a")
        b = kk * a
        e_inv, e_end = v.pop("e_inv"), v.pop("e_end")
        v["A_t"] = -kk * v.pop("e_ex")
        v["R_t"] = v["r"] * v.pop("e_in")
        v["B_t"] = b * e_inv
        v["K_t"] = k2 * e_inv
        v["B_e"] = b * e_end
        v["K_e"] = k2 * e_end

    return [shift("lo", 3 * RWKV_DIM, 3 * LORA), lora, shift("k", RWKV_DIM, RWKV_DIM), keys, decay,
            shift("r", 0, RWKV_DIM), shift("v", 2 * RWKV_DIM, RWKV_DIM), products]


def _wkv_stages(s, pins, never):
    C = CHUNK
    v = s.v
    pairs = range(PAIRS)
    sl = [slice(p * LANES, (p + 1) * LANES) for p in pairs]
    low2 = lax.broadcasted_iota(jnp.int32, (2 * C, LANES), 1) < HEAD_SIZE
    low1 = lax.broadcasted_iota(jnp.int32, (C, LANES), 1) < HEAD_SIZE
    r4 = lax.broadcasted_iota(jnp.int32, (C, 4 * C), 0)
    c4 = lax.broadcasted_iota(jnp.int32, (C, 4 * C), 1) & (C - 1)
    strict = c4 < r4
    incl = c4 <= r4
    rr = lax.broadcasted_iota(jnp.int32, (LANES, LANES), 0)
    cc = lax.broadcasted_iota(jnp.int32, (LANES, LANES), 1)
    same_head = (rr < HEAD_SIZE) == (cc < HEAD_SIZE)
    z_tile = jnp.zeros((C, LANES), BF16)
    z_rows = jnp.zeros((C, 2 * LANES), BF16)
    w = {}

    def masked_rows(p):
        u = w["U"][p]
        return jnp.concatenate([jnp.where(low1, u, 0.0).astype(BF16), w["v_low"][p],
                                jnp.where(low1, 0.0, u).astype(BF16), w["v_high"][p]], axis=0)

    def gram():
        w["S0"] = [s.s[p] for p in pairs]
        w["v_low"] = [jnp.where(low1, v["v"][:, sl[p]], 0.0).astype(BF16) for p in pairs]
        w["v_high"] = [jnp.where(low1, 0.0, v["v"][:, sl[p]]).astype(BF16) for p in pairs]
        A_t, R_t, B_t, K_t = v.pop("A_t"), v.pop("R_t"), v.pop("B_t"), v.pop("K_t")
        GG = []
        for p in pairs:
            AR = jnp.concatenate([A_t[:, sl[p]], R_t[:, sl[p]]], axis=0)
            lhs = jnp.concatenate([jnp.where(low2, AR, 0.0), jnp.where(low2, 0.0, AR)], axis=0)
            rhs = jnp.concatenate([B_t[:, sl[p]], K_t[:, sl[p]], w["S0"][p]], axis=0)
            GG.append(_bdot_nt(lhs, rhs))
        side = lambda g, r0: jnp.concatenate(
            [g[r0:r0 + C, 0:2 * C], g[r0 + 2 * C:r0 + 3 * C, 0:2 * C]], axis=-1)
        w["P"] = [jnp.where(strict, side(g, 0), 0.0).astype(BF16) for g in GG]
        w["M"] = [jnp.where(incl, side(g, C), 0.0).astype(BF16) for g in GG]
        w["U"] = [g[0:C, 2 * C:] + g[2 * C:3 * C, 2 * C:] for g in GG]
        w["AS_bot"] = [g[C:2 * C, 2 * C:] + g[3 * C:4 * C, 2 * C:] for g in GG]

    def take_pins():
        while pins:
            w["U"][PAIRS - 1] = jnp.where(never, pins.pop(), w["U"][PAIRS - 1])

    def level(square):
        def f():
            take_pins()
            Z = [masked_rows(p) for p in pairs]
            P = w["P"]
            w["U"] = [w["U"][p] + jnp.dot(P[p], Z[p], preferred_element_type=F32) for p in pairs]
            if square:
                sq = [jnp.concatenate([
                    jnp.concatenate([P[p][:, 0:2 * C], z_tile], axis=-1), z_rows,
                    jnp.concatenate([z_tile, P[p][:, 2 * C:]], axis=-1), z_rows], axis=0)
                    for p in pairs]
                w["P"] = [jnp.dot(P[p], sq[p], preferred_element_type=F32).astype(BF16)
                          for p in pairs]
        return f

    def finish():
        take_pins()
        B_e, K_e, g_tot = v.pop("B_e"), v.pop("K_e"), v.pop("g_tot")
        ys = []
        for p in pairs:
            ys.append(w["AS_bot"][p]
                      + jnp.dot(w["M"][p], masked_rows(p), preferred_element_type=F32))
            UV = jnp.concatenate([w["U"][p], v["v"][:, sl[p]]], axis=0)
            BKe = jnp.concatenate([B_e[:, sl[p]], K_e[:, sl[p]]], axis=0)
            upd = _bdot(UV.T, BKe)
            s.s[p] = w["S0"][p] * g_tot[:, sl[p]] + jnp.where(same_head, upd, 0.0)
        v["y"] = jnp.concatenate(ys, axis=-1)
        w.clear()

    return [gram] + [level(True) for _ in range(SOLVE_SQUARINGS)] + [level(False), finish]


def _rwkv_post_pieces(s, W, ones_bd):
    v = s.v
    inv_n = 1.0 / HEAD_SIZE

    def center():
        y = v.pop("y")
        v["yc"] = y - _head_sums(y, ones_bd) * inv_n

    def scale():
        yc = v.pop("yc")
        yv = _head_sums(yc * yc, ones_bd) * inv_n
        v["yn"] = yc * lax.rsqrt(yv + GN_EPS) * W.gn_g[...] + W.gn_b[...]

    def bonus():
        v["bonus"] = _head_sums(v.pop("r") * v.pop("k2") * W.r_k[...], ones_bd) * v.pop("v")

    def gate():
        out = (v.pop("yn") + v.pop("bonus")) * v.pop("g")
        s.mix[:, CONV_DIM + POOL_DIM:] = out.astype(BF16)
        return _lane_tile_sum(out)

    return [center, scale, bonus, gate]


def _run_interleaved(stages, fillers, pins):
    n = len(stages)
    for i, stage in enumerate(stages):
        stage()
        for f in fillers[i * len(fillers) // n:(i + 1) * len(fillers) // n]:
            token = f()
            if token is not None:
                pins.append(token)


_MixerWeights = collections.namedtuple(
    "_MixerWeights", "conv_w conv_b ln_g ln_b pool_w pool_scale mu w0 w_up a0 a_up g_up k_k k_a "
                     "r_k gn_g gn_b")


def _mixer_kernel(start_pos, z_ref, hc_ref, hp_ref, hs_ref, s0_ref, ones_ref, *rest):
    n_w = len(_MixerWeights._fields)
    W = _MixerWeights(*rest[:n_w])
    mix_ref, newc_ref, newp_ref, news_ref, news_wkv_ref, ubuf, pbuf, qprev, s_scr = rest[n_w:]
    C = CHUNK
    t = pl.program_id(1)

    @pl.when(t == 0)
    def _():
        zero = jnp.zeros((HEAD_SIZE, HEAD_SIZE), F32)
        for i in range(SEQS_PER_STEP):
            ubuf[i, 0:2, :] = jnp.zeros((2, CONV_DIM), F32)
            ubuf[i, 2:32, :] = hc_ref[i]
            pbuf[i, 0:1, :] = jnp.zeros((1, POOL_DIM), F32)
            pbuf[i, 1:16, :] = hp_ref[i]
            qprev[i] = jnp.zeros(qprev.shape[1:], F32)
            qprev[i, 0:1, :] = hs_ref[i]
            for p in range(PAIRS):
                top = jnp.concatenate([s0_ref[i, 2 * p], zero], axis=-1)
                bot = jnp.concatenate([zero, s0_ref[i, 2 * p + 1]], axis=-1)
                s_scr[i, p] = jnp.concatenate([top, bot], axis=0)

    ones_bd = ones_ref[...]
    pos0 = start_pos + t * C
    seqs = [_Seq(i, z_ref, ubuf, pbuf, qprev, s_scr, mix_ref) for i in range(SEQS_PER_STEP)]
    prep = [_rwkv_prep_pieces(s, W, ones_bd) for s in seqs]
    side = [_conv_pool_pieces(s, W, pos0) for s in seqs]
    post = [_rwkv_post_pieces(s, W, ones_bd) for s in seqs]
    pins = []
    never = t < 0
    rounds = [range(i, i + SEQS_PER_ROUND) for i in range(0, SEQS_PER_STEP, SEQS_PER_ROUND)]
    gather = lambda pieces, ids: [f for i in ids for f in pieces[i]]
    for f in gather(prep, rounds[0]):
        f()
    for r, ids in enumerate(rounds):
        stages = [st for group in zip(*[_wkv_stages(seqs[i], pins, never) for i in ids])
                  for st in group]
        fillers = gather(side, ids)
        if r + 1 < len(rounds):
            fillers += gather(prep, rounds[r + 1])
        if r:
            fillers += gather(post, rounds[r - 1])
        _run_interleaved(stages, fillers, pins)
    for f in gather(post, rounds[-1]):
        f()

    @pl.when(t == pl.num_programs(1) - 1)
    def _():
        for i in range(SEQS_PER_STEP):
            newc_ref[i] = ubuf[i, 2:32, :]
            newp_ref[i] = pbuf[i, 1:16, :]
            news_ref[i] = qprev[i, 0:1, :]
            for p in range(PAIRS):
                sp = s_scr[i, p]
                news_wkv_ref[i, 2 * p] = sp[0:HEAD_SIZE, 0:HEAD_SIZE]
                news_wkv_ref[i, 2 * p + 1] = sp[HEAD_SIZE:, HEAD_SIZE:]


def _mixers(z, hist_conv, hist_pool, hist_shift, state_wkv, ones_bd, wts, layer, start_pos):
    B, L, _ = z.shape
    C, G = CHUNK, SEQS_PER_STEP
    assert L % C == 0 and B % G == 0

    def per_layer(shape):
        nd = len(shape)
        return pl.BlockSpec((None,) + shape, lambda b, t: (layer,) + (0,) * nd)

    def per_seq(shape):
        nd = len(shape)
        return pl.BlockSpec((None, G) + shape, lambda b, t: (layer, b) + (0,) * nd)

    def out_seq(shape):
        nd = len(shape)
        return pl.BlockSpec((G,) + shape, lambda b, t: (b,) + (0,) * nd)

    vec = lambda n: per_layer((1, n))
    in_specs = [
        pl.BlockSpec((G, C, IN_PROJ), lambda b, t: (b, t, 0)),
        per_seq((CONV_HIST, CONV_DIM)),
        per_seq((POOL_HIST, POOL_DIM)),
        per_seq((1, RWKV_PROJ)),
        per_seq((RWKV_HEADS, HEAD_SIZE, HEAD_SIZE)),
        pl.BlockSpec((LANES, LANES), lambda b, t: (0, 0)),
        per_layer((CONV_WIDTH, CONV_DIM)), vec(CONV_DIM), vec(CONV_DIM), vec(CONV_DIM),
        per_layer((len(POOL_WINDOWS), POOL_GROUP_DIM, POOL_GROUP_DIM)), vec(POOL_DIM),
        vec(RWKV_PROJ), vec(RWKV_DIM), per_layer((LORA, RWKV_DIM)), vec(RWKV_DIM),
        per_layer((LORA, RWKV_DIM)), per_layer((LORA, RWKV_DIM)),
        vec(RWKV_DIM), vec(RWKV_DIM), vec(RWKV_DIM), vec(RWKV_DIM), vec(RWKV_DIM),
    ]
    out_specs = [
        pl.BlockSpec((G, C, D_MODEL), lambda b, t: (b, t, 0)),
        out_seq((CONV_HIST, CONV_DIM)),
        out_seq((POOL_HIST, POOL_DIM)),
        out_seq((1, RWKV_PROJ)),
        out_seq((RWKV_HEADS, HEAD_SIZE, HEAD_SIZE)),
    ]
    out_shape = [
        jax.ShapeDtypeStruct((B, L, D_MODEL), BF16),
        jax.ShapeDtypeStruct((B, CONV_HIST, CONV_DIM), F32),
        jax.ShapeDtypeStruct((B, POOL_HIST, POOL_DIM), F32),
        jax.ShapeDtypeStruct((B, 1, RWKV_PROJ), F32),
        jax.ShapeDtypeStruct((B, RWKV_HEADS, HEAD_SIZE, HEAD_SIZE), F32),
    ]
    scratch = [
        pltpu.VMEM((G, 32 + C, CONV_DIM), F32),
        pltpu.VMEM((G, 16 + C, POOL_DIM), F32),
        pltpu.VMEM((G, SUBLANES, RWKV_PROJ), F32),
        pltpu.VMEM((G, PAIRS, LANES, LANES), F32),
    ]
    return pl.pallas_call(
        functools.partial(_mixer_kernel, start_pos),
        grid=(B // G, L // C),
        in_specs=in_specs,
        out_specs=out_specs,
        out_shape=out_shape,
        scratch_shapes=scratch,
        compiler_params=pltpu.CompilerParams(
            dimension_semantics=("arbitrary", "arbitrary"), vmem_limit_bytes=VMEM_LIMIT),
    )(z, hist_conv, hist_pool, hist_shift, state_wkv, ones_bd, *wts)


def _prepare_weights(weights):
    weights = list(weights)
    w_in, w_out, ffn_down = weights[1], weights[19], weights[23]
    weights[1] = _transpose_cast_weight(jnp.swapaxes(w_in, 1, 2), IN_PROJ_PAD, 1024)
    weights[19] = _cast_weight(w_out, 512)
    weights[23] = _cast_weight(ffn_down, 512)
    return tuple(weights)


def _trunk(x, hist_conv, hist_pool, hist_shift, state_wkv, start_pos, weights):
    (norm_mix, w_in, conv_w, conv_b, conv_ln_g, conv_ln_b, pool_w, pool_scale,
     shift_mu, decay_w0, decay_up, iclr_a0, iclr_up, gate_up, k_k, k_a, r_k, gn_g, gn_b,
     w_out, norm_ffn, ffn_gate, ffn_up, ffn_down, norm_final) = weights
    B, L, _ = x.shape
    T = B * L
    as_rows = lambda a: a.reshape(DEPTH, 1, a.shape[-1])
    mixer_wts = (conv_w, as_rows(conv_b), as_rows(conv_ln_g), as_rows(conv_ln_b), pool_w,
                 as_rows(pool_scale), as_rows(shift_mu), as_rows(decay_w0), decay_up,
                 as_rows(iclr_a0), iclr_up, gate_up, as_rows(k_k), as_rows(k_a),
                 as_rows(r_k.reshape(DEPTH, RWKV_DIM)), as_rows(gn_g), as_rows(gn_b))
    norm_mix3 = as_rows(norm_mix)
    idx = jnp.arange(LANES) // HEAD_SIZE
    ones_bd = (idx[:, None] == idx[None, :]).astype(BF16)

    x = x.reshape(T, D_MODEL)
    h = x
    convs, pools, shifts, wkvs = [], [], [], []
    for layer in range(DEPTH):
        last = layer == DEPTH - 1
        z = _in_proj(h, norm_mix3, w_in, layer, fuse_norm=layer == 0)
        mix, c_new, p_new, s_new, S_new = _mixers(
            z.reshape(B, L, IN_PROJ), hist_conv, hist_pool, hist_shift, state_wkv, ones_bd,
            mixer_wts, layer, start_pos)
        x, h = _residual_proj(x, mix.reshape(T, D_MODEL), w_out, layer, norm_ffn[layer][None],
                              final=False)
        act = _ffn_up(h, ffn_gate, ffn_up, layer)
        gain = norm_final[None] if last else norm_mix[layer + 1][None]
        out = _residual_proj(x, act, ffn_down, layer, gain, final=last)
        if not last:
            x, h = out
        convs.append(c_new)
        pools.append(p_new)
        shifts.append(s_new)
        wkvs.append(S_new)
    y = out.reshape(B, L, D_MODEL)
    return y, jnp.stack(convs), jnp.stack(pools), jnp.stack(shifts), jnp.stack(wkvs)


def kernel(x_prompt, x_sample, cache_conv, cache_pool, state_shift, state_wkv, norm_mix, w_in,
           conv_w, conv_b, conv_ln_g, conv_ln_b, pool_w, pool_scale, shift_mu, decay_w0, decay_up,
           iclr_a0, iclr_up, gate_up, k_k, k_a, r_k, gn_g, gn_b, w_out, norm_ffn, ffn_gate,
           ffn_up, ffn_down, norm_final):
    weights = (norm_mix, w_in, conv_w, conv_b, conv_ln_g, conv_ln_b, pool_w, pool_scale,
               shift_mu, decay_w0, decay_up, iclr_a0, iclr_up, gate_up, k_k, k_a, r_k, gn_g, gn_b,
               w_out, norm_ffn, ffn_gate, ffn_up, ffn_down, norm_final)
    weights = _prepare_weights(weights)
    bp = x_prompt.shape[0]
    zc = jnp.zeros((DEPTH, bp, CONV_HIST, CONV_DIM), F32)
    zp = jnp.zeros((DEPTH, bp, POOL_HIST, POOL_DIM), F32)
    zs = jnp.zeros((DEPTH, bp, 1, RWKV_PROJ), F32)
    zw = jnp.zeros((DEPTH, bp, RWKV_HEADS, HEAD_SIZE, HEAD_SIZE), F32)
    y_p, p_conv, p_pool, p_shift, p_wkv = _trunk(x_prompt, zc, zp, zs, zw, 0, weights)
    y_s, s_conv, s_pool, s_shift, s_wkv = _trunk(x_sample, cache_conv, cache_pool, state_shift,
                                                 state_wkv, PAST_LEN, weights)
    return (y_p, y_s, p_conv, p_pool, p_shift, p_wkv, s_conv, s_pool, s_shift, s_wkv)
```

```python
import collections
import functools
import math

import jax
import jax.numpy as jnp
from jax import lax
from jax.experimental import pallas as pl
from jax.experimental.pallas import tpu as pltpu

F32 = jnp.float32
BF16 = jnp.bfloat16

D_MODEL = 2048
DEPTH = 4
PAST_LEN = 4096
CONV_DIM = 512
CONV_WIDTH = 31
CONV_HIST = CONV_WIDTH - 1
POOL_DIM = 512
POOL_WINDOWS = (2, 4, 8, 16)
POOL_GROUP_DIM = 128
POOL_HIST = max(POOL_WINDOWS) - 1
RWKV_DIM = 1024
HEAD_SIZE = 64
RWKV_HEADS = RWKV_DIM // HEAD_SIZE
LORA = 64
RWKV_PROJ = 3 * RWKV_DIM + 3 * LORA
IN_PROJ = 2 * CONV_DIM + POOL_DIM + RWKV_PROJ
POOL_OFF = 2 * CONV_DIM
RWKV_OFF = POOL_OFF + POOL_DIM
D_FF = 5632
RMS_EPS = 1e-6
DECAY_SCALE = math.exp(-0.5)
LN_EPS = 1e-5
GN_EPS = 64e-5

LANES = 128
SUBLANES = 8
PAIRS = RWKV_DIM // LANES
SEQS_PER_STEP = 4
MAX_CHUNKS_PER_STEP = 2
UNITS_PER_ROUND = 2
CHUNK = 64
SOLVE_SQUARINGS = 5
VMEM_LIMIT = 56 * 1024 * 1024


def _tiles(n_tokens):
    tm = min(n_tokens, 1024)
    assert n_tokens % tm == 0
    return dict(tm=tm, tn=1024, tn_ff=512, tk_res=512)


IN_PROJ_PAD = -(-IN_PROJ // 1024) * 1024


def _cast_kernel(w_ref, o_ref):
    o_ref[...] = w_ref[...].astype(BF16)


def _cast_weight(w, rows):
    depth, K, N = w.shape
    return pl.pallas_call(
        _cast_kernel,
        grid=(depth, K // rows),
        in_specs=[pl.BlockSpec((None, rows, N), lambda l, k: (l, k, 0))],
        out_specs=pl.BlockSpec((None, rows, N), lambda l, k: (l, k, 0)),
        out_shape=jax.ShapeDtypeStruct(w.shape, BF16),
        compiler_params=pltpu.CompilerParams(dimension_semantics=("arbitrary", "arbitrary")),
    )(w)


def _transpose_cast_kernel(n_valid, w_ref, o_ref):
    rows = w_ref.shape[0]
    row = pl.program_id(1) * rows + lax.broadcasted_iota(jnp.int32, w_ref.shape, 0)
    o_ref[...] = jnp.where(row < n_valid, w_ref[...], 0.0).T.astype(BF16)


def _transpose_cast_weight(w_t, n_pad, cols):
    depth, N, K = w_t.shape
    return pl.pallas_call(
        functools.partial(_transpose_cast_kernel, N),
        grid=(depth, n_pad // cols),
        in_specs=[pl.BlockSpec((None, cols, K), lambda l, n: (l, n, 0))],
        out_specs=pl.BlockSpec((None, K, cols), lambda l, n: (l, 0, n)),
        out_shape=jax.ShapeDtypeStruct((depth, K, n_pad), BF16),
        compiler_params=pltpu.CompilerParams(dimension_semantics=("arbitrary", "arbitrary")),
    )(w_t)


def _bdot(a, b):
    return jnp.dot(a.astype(BF16), b.astype(BF16), preferred_element_type=F32)


def _bdot_nt(a, b):
    return lax.dot_general(a.astype(BF16), b.astype(BF16), (((1,), (1,)), ((), ())),
                           preferred_element_type=F32)


def _sigmoid(x):
    return 1.0 / (1.0 + jnp.exp(-x))


def _rms_norm(x, gain):
    ms = jnp.mean(x * x, axis=-1, keepdims=True)
    return x * lax.rsqrt(ms + RMS_EPS) * gain


def _in_proj_kernel(fuse_norm, x_ref, g_ref, w_ref, o_ref, *scratch):
    if fuse_norm:
        h_ref, = scratch

        @pl.when(pl.program_id(1) == 0)
        def _():
            h_ref[...] = _rms_norm(x_ref[...], g_ref[...]).astype(BF16)

        h = h_ref[...]
    else:
        h = x_ref[...]
    o_ref[...] = jnp.dot(h, w_ref[...], preferred_element_type=F32)


def _in_proj(x, g, w, layer, fuse_norm):
    T = x.shape[0]
    t = _tiles(T)
    tm, tn = t["tm"], t["tn"]
    return pl.pallas_call(
        functools.partial(_in_proj_kernel, fuse_norm),
        grid=(T // tm, w.shape[-1] // tn),
        in_specs=[
            pl.BlockSpec((tm, D_MODEL), lambda m, n: (m, 0)),
            pl.BlockSpec((None, 1, D_MODEL), lambda m, n: (layer, 0, 0)),
            pl.BlockSpec((None, D_MODEL, tn), lambda m, n: (layer, 0, n)),
        ],
        out_specs=pl.BlockSpec((tm, tn), lambda m, n: (m, n)),
        out_shape=jax.ShapeDtypeStruct((T, IN_PROJ), F32),
        scratch_shapes=[pltpu.VMEM((tm, D_MODEL), BF16)] if fuse_norm else [],
        compiler_params=pltpu.CompilerParams(
            dimension_semantics=("arbitrary", "arbitrary"), vmem_limit_bytes=VMEM_LIMIT),
    )(x, g, w)


def _ffn_up_kernel(h_ref, wg_ref, wu_ref, o_ref):
    h = h_ref[...]
    gate = jnp.dot(h, wg_ref[...].astype(BF16), preferred_element_type=F32)
    up = jnp.dot(h, wu_ref[...].astype(BF16), preferred_element_type=F32)
    o_ref[...] = (gate * _sigmoid(gate) * up).astype(BF16)


def _ffn_up(h, wg, wu, layer):
    T = h.shape[0]
    t = _tiles(T)
    tm, tn = t["tm"], t["tn_ff"]
    return pl.pallas_call(
        _ffn_up_kernel,
        grid=(T // tm, D_FF // tn),
        in_specs=[
            pl.BlockSpec((tm, D_MODEL), lambda m, n: (m, 0)),
            pl.BlockSpec((None, D_MODEL, tn), lambda m, n: (layer, 0, n)),
            pl.BlockSpec((None, D_MODEL, tn), lambda m, n: (layer, 0, n)),
        ],
        out_specs=pl.BlockSpec((tm, tn), lambda m, n: (m, n)),
        out_shape=jax.ShapeDtypeStruct((T, D_FF), BF16),
        compiler_params=pltpu.CompilerParams(
            dimension_semantics=("arbitrary", "arbitrary"), vmem_limit_bytes=VMEM_LIMIT),
    )(h, wg, wu)


def _residual_proj_kernel(final, x_ref, a_ref, w_ref, g_ref, o_ref, *h_ref):
    k = pl.program_id(1)

    @pl.when(k == 0)
    def _():
        o_ref[...] = x_ref[...]

    o_ref[...] += jnp.dot(a_ref[...], w_ref[...], preferred_element_type=F32)

    @pl.when(k == pl.num_programs(1) - 1)
    def _():
        normed = _rms_norm(o_ref[...], g_ref[...])
        if final:
            o_ref[...] = normed
        else:
            h_ref[0][...] = normed.astype(BF16)


def _residual_proj(x, a, w, layer, gain, final):
    T, K = a.shape
    t = _tiles(T)
    tm, tk = t["tm"], t["tk_res"]
    row_spec = pl.BlockSpec((tm, D_MODEL), lambda m, k: (m, 0))
    out_shape = [jax.ShapeDtypeStruct((T, D_MODEL), F32)]
    if not final:
        out_shape.append(jax.ShapeDtypeStruct((T, D_MODEL), BF16))
    out = pl.pallas_call(
        functools.partial(_residual_proj_kernel, final),
        grid=(T // tm, K // tk),
        in_specs=[
            row_spec,
            pl.BlockSpec((tm, tk), lambda m, k: (m, k)),
            pl.BlockSpec((None, tk, D_MODEL), lambda m, k: (layer, k, 0)),
            pl.BlockSpec((1, D_MODEL), lambda m, k: (0, 0)),
        ],
        out_specs=[row_spec] * len(out_shape),
        out_shape=out_shape,
        compiler_params=pltpu.CompilerParams(
            dimension_semantics=("arbitrary", "arbitrary"), vmem_limit_bytes=VMEM_LIMIT),
    )(x, a, w, gain)
    return out[0] if final else out


class _Unit:
    def __init__(self, i, c, z_ref, ubuf, pbuf, qprev, s_scr, mix_ref):
        rows = pl.ds(c * CHUNK, CHUNK)
        self.z = z_ref.at[i, rows]
        self.ubuf = ubuf.at[i]
        self.pbuf = pbuf.at[i]
        self.qprev = qprev.at[i]
        self.s = s_scr.at[i]
        self.mix = mix_ref.at[i, rows]
        self.v = {}


def _head_sums(x, ones_bd):
    stacked = jnp.concatenate([x[:, p * LANES:(p + 1) * LANES] for p in range(PAIRS)], axis=0)
    s = _bdot(stacked, ones_bd)
    C = x.shape[0]
    return jnp.concatenate([s[p * C:(p + 1) * C, :] for p in range(PAIRS)], axis=-1)


def _lane_tile_sum(x):
    tiles = [x[:, i * LANES:(i + 1) * LANES] for i in range(x.shape[-1] // LANES)]
    while len(tiles) > 1:
        tiles = [a + b for a, b in zip(tiles[0::2], tiles[1::2])] + tiles[len(tiles) & ~1:]
    return tiles[0]


def _conv_pool_pieces(s, W, pos0):
    C = CHUNK
    v = s.v

    def glu():
        val = s.z[:, 0:CONV_DIM]
        gate = s.z[:, CONV_DIM:2 * CONV_DIM]
        s.ubuf[32:32 + C, :] = val * _sigmoid(gate)

    def taps(tile):
        def f():
            lanes = slice(tile * LANES, (tile + 1) * LANES)
            rows = 32 + C
            full = s.ubuf[:, lanes]
            shifted = [full] + [pltpu.roll(full, rows - k, 0) for k in range(1, SUBLANES)]
            acc = jnp.zeros((C, LANES), F32) + W.conv_b[:, lanes]
            for j in range(CONV_WIDTH):
                phase, base = (2 + j) % SUBLANES, (2 + j) // SUBLANES * SUBLANES
                acc = acc + shifted[phase][base:base + C, :] * W.conv_w[j:j + 1, lanes]
            v["conv%d" % tile] = acc
            return acc
        return f

    def norm():
        acc = jnp.concatenate([v.pop("conv%d" % t) for t in range(CONV_DIM // LANES)], axis=-1)
        mu = jnp.mean(acc, axis=-1, keepdims=True)
        cen = acc - mu
        var = jnp.mean(cen * cen, axis=-1, keepdims=True)
        hn = cen * lax.rsqrt(var + LN_EPS) * W.ln_g[...] + W.ln_b[...]
        out = hn * _sigmoid(hn)
        s.mix[:, 0:CONV_DIM] = out.astype(BF16)
        s.ubuf[2:32, :] = s.ubuf[C + 2:C + 32, :]
        return _lane_tile_sum(out)

    def pool():
        s.pbuf[16:16 + C, :] = s.z[:, POOL_OFF:POOL_OFF + POOL_DIM]
        pos = pos0 + lax.broadcasted_iota(jnp.int32, (C, POOL_GROUP_DIM), 0)
        outs = []
        for gi, w in enumerate(POOL_WINDOWS):
            lo, hi = gi * POOL_GROUP_DIM, (gi + 1) * POOL_GROUP_DIM
            tok = s.pbuf[16:16 + C, lo:hi]
            tot = tok
            for j in range(1, w):
                tot = tot + s.pbuf[16 - j:16 - j + C, lo:hi]
            cnt = jnp.minimum(w, pos + 1).astype(F32)
            d = tot / cnt - tok
            outs.append(_bdot(d, W.pool_w[gi]))
        out = jnp.concatenate(outs, axis=-1) * W.pool_scale[...]
        s.mix[:, CONV_DIM:CONV_DIM + POOL_DIM] = out.astype(BF16)
        s.pbuf[1:16, :] = s.pbuf[C + 1:C + 16, :]
        return _lane_tile_sum(out)

    return [glu] + [taps(t) for t in range(CONV_DIM // LANES)] + [norm, pool]


def _rwkv_prep_pieces(s, W, ones_bd):
    C = CHUNK
    v = s.v

    def shift(name, off, width):
        def f():
            q = s.z[:, RWKV_OFF + off:RWKV_OFF + off + width]
            rolled = pltpu.roll(q, 1, 0)
            first_row = lax.broadcasted_iota(jnp.int32, (C, width), 0) == 0
            q_prev = jnp.where(first_row, s.qprev[0:1, off:off + width], rolled)
            v[name] = q + (q_prev - q) * W.mu[:, off:off + width]
            s.qprev[0:1, off:off + width] = q[C - 1:C, :]
        return f

    def lora():
        lo = v.pop("lo")
        w_lo, a_lo, g_lo = lo[:, 0:LORA], lo[:, LORA:2 * LORA], lo[:, 2 * LORA:3 * LORA]
        x = W.w0[...] + _bdot(jnp.tanh(w_lo), W.w_up[...])
        v["lw"] = -DECAY_SCALE * _sigmoid(x)
        v["a"] = _sigmoid(W.a0[...] + _bdot(a_lo, W.a_up[...]))
        v["g"] = _bdot(_sigmoid(g_lo), W.g_up[...])

    def keys():
        k, a = v.pop("k"), v["a"]
        kk = k * W.k_k[...]
        kk = kk * lax.rsqrt(jnp.maximum(_head_sums(kk * kk, ones_bd), 1e-24))
        v["kk"] = kk
        v["k2"] = k * (1.0 + (a - 1.0) * W.k_a[...])

    def decay():
        lw = v.pop("lw")
        row = lax.broadcasted_iota(jnp.int32, (C, C), 0)
        col = lax.broadcasted_iota(jnp.int32, (C, C), 1)
        tri = (col <= row).astype(BF16)
        lw_hi = lw.astype(BF16)
        rem = lw - lw_hi.astype(F32)
        lw_mid = rem.astype(BF16)
        lw_lo = (rem - lw_mid.astype(F32)).astype(BF16)
        cum = (jnp.dot(tri, lw_hi, preferred_element_type=F32)
               + jnp.dot(tri, lw_mid, preferred_element_type=F32)
               + jnp.dot(tri, lw_lo, preferred_element_type=F32))
        tot = cum[C - 1:C, :]
        v["e_ex"] = jnp.exp(cum - lw)
        v["e_in"] = jnp.exp(cum)
        v["e_inv"] = jnp.exp(-cum)
        v["e_end"] = jnp.exp(tot - cum)
        v["g_tot"] = jnp.exp(tot)

    def products():
        kk, k2, a = v.pop("kk"), v["k2"], v.pop("a")
        b = kk * a
        e_inv, e_end = v.pop("e_inv"), v.pop("e_end")
        v["A_t"] = -kk * v.pop("e_ex")
        v["R_t"] = v["r"] * v.pop("e_in")
        v["B_t"] = b * e_inv
        v["K_t"] = k2 * e_inv
        v["B_e"] = b * e_end
        v["K_e"] = k2 * e_end

    return [shift("lo", 3 * RWKV_DIM, 3 * LORA), lora, shift("k", RWKV_DIM, RWKV_DIM), keys, decay,
            shift("r", 0, RWKV_DIM), shift("v", 2 * RWKV_DIM, RWKV_DIM), products]


def _wkv_stages(s, pins, never):
    C = CHUNK
    v = s.v
    pairs = range(PAIRS)
    sl = [slice(p * LANES, (p + 1) * LANES) for p in pairs]
    low2 = lax.broadcasted_iota(jnp.int32, (2 * C, LANES), 1) < HEAD_SIZE
    low1 = lax.broadcasted_iota(jnp.int32, (C, LANES), 1) < HEAD_SIZE
    r4 = lax.broadcasted_iota(jnp.int32, (C, 4 * C), 0)
    c4 = lax.broadcasted_iota(jnp.int32, (C, 4 * C), 1) & (C - 1)
    strict = c4 < r4
    incl = c4 <= r4
    rr = lax.broadcasted_iota(jnp.int32, (LANES, LANES), 0)
    cc = lax.broadcasted_iota(jnp.int32, (LANES, LANES), 1)
    same_head = (rr < HEAD_SIZE) == (cc < HEAD_SIZE)
    z_tile = jnp.zeros((C, LANES), BF16)
    z_rows = jnp.zeros((C, 2 * LANES), BF16)
    w = {}

    def masked_rows(p):
        u = w["U"][p]
        return jnp.concatenate([jnp.where(low1, u, 0.0).astype(BF16), w["v_low"][p],
                                jnp.where(low1, 0.0, u).astype(BF16), w["v_high"][p]], axis=0)

    def gram():
        w["S0"] = [s.s[p] for p in pairs]
        w["v_low"] = [jnp.where(low1, v["v"][:, sl[p]], 0.0).astype(BF16) for p in pairs]
        w["v_high"] = [jnp.where(low1, 0.0, v["v"][:, sl[p]]).astype(BF16) for p in pairs]
        A_t, R_t, B_t, K_t = v.pop("A_t"), v.pop("R_t"), v.pop("B_t"), v.pop("K_t")
        GG = []
        for p in pairs:
            AR = jnp.concatenate([A_t[:, sl[p]], R_t[:, sl[p]]], axis=0)
            lhs = jnp.concatenate([jnp.where(low2, AR, 0.0), jnp.where(low2, 0.0, AR)], axis=0)
            rhs = jnp.concatenate([B_t[:, sl[p]], K_t[:, sl[p]], w["S0"][p]], axis=0)
            GG.append(_bdot_nt(lhs, rhs))
        side = lambda g, r0: jnp.concatenate(
            [g[r0:r0 + C, 0:2 * C], g[r0 + 2 * C:r0 + 3 * C, 0:2 * C]], axis=-1)
        w["P"] = [jnp.where(strict, side(g, 0), 0.0).astype(BF16) for g in GG]
        w["M"] = [jnp.where(incl, side(g, C), 0.0).astype(BF16) for g in GG]
        w["U"] = [g[0:C, 2 * C:] + g[2 * C:3 * C, 2 * C:] for g in GG]
        w["AS_bot"] = [g[C:2 * C, 2 * C:] + g[3 * C:4 * C, 2 * C:] for g in GG]

    def take_pins():
        while pins:
            w["U"][PAIRS - 1] = jnp.where(never, pins.pop(), w["U"][PAIRS - 1])

    def level(square):
        def f():
            take_pins()
            Z = [masked_rows(p) for p in pairs]
            P = w["P"]
            w["U"] = [w["U"][p] + jnp.dot(P[p], Z[p], preferred_element_type=F32) for p in pairs]
            if square:
                sq = [jnp.concatenate([
                    jnp.concatenate([P[p][:, 0:2 * C], z_tile], axis=-1), z_rows,
                    jnp.concatenate([z_tile, P[p][:, 2 * C:]], axis=-1), z_rows], axis=0)
                    for p in pairs]
                w["P"] = [jnp.dot(P[p], sq[p], preferred_element_type=F32).astype(BF16)
                          for p in pairs]
        return f

    def finish():
        take_pins()
        B_e, K_e, g_tot = v.pop("B_e"), v.pop("K_e"), v.pop("g_tot")
        ys = []
        for p in pairs:
            ys.append(w["AS_bot"][p]
                      + jnp.dot(w["M"][p], masked_rows(p), preferred_element_type=F32))
            UV = jnp.concatenate([w["U"][p], v["v"][:, sl[p]]], axis=0)
            BKe = jnp.concatenate([B_e[:, sl[p]], K_e[:, sl[p]]], axis=0)
            upd = _bdot(UV.T, BKe)
            s.s[p] = w["S0"][p] * g_tot[:, sl[p]] + jnp.where(same_head, upd, 0.0)
        v["y"] = jnp.concatenate(ys, axis=-1)
        w.clear()

    return [gram] + [level(True) for _ in range(SOLVE_SQUARINGS)] + [level(False), finish]


def _rwkv_post_pieces(s, W, ones_bd):
    v = s.v
    inv_n = 1.0 / HEAD_SIZE

    def center():
        y = v.pop("y")
        v["yc"] = y - _head_sums(y, ones_bd) * inv_n

    def scale():
        yc = v.pop("yc")
        yv = _head_sums(yc * yc, ones_bd) * inv_n
        v["yn"] = yc * lax.rsqrt(yv + GN_EPS) * W.gn_g[...] + W.gn_b[...]

    def bonus():
        v["bonus"] = _head_sums(v.pop("r") * v.pop("k2") * W.r_k[...], ones_bd) * v.pop("v")

    def gate():
        out = (v.pop("yn") + v.pop("bonus")) * v.pop("g")
        s.mix[:, CONV_DIM + POOL_DIM:] = out.astype(BF16)
        return _lane_tile_sum(out)

    return [center, scale, bonus, gate]


def _run_interleaved(stages, fillers, pins):
    n = len(stages)
    for i, stage in enumerate(stages):
        stage()
        for f in fillers[i * len(fillers) // n:(i + 1) * len(fillers) // n]:
            token = f()
            if token is not None:
                pins.append(token)


_MixerWeights = collections.namedtuple(
    "_MixerWeights", "conv_w conv_b ln_g ln_b pool_w pool_scale mu w0 w_up a0 a_up g_up k_k k_a "
                     "r_k gn_g gn_b")


def _mixer_kernel(start_pos, n_seqs, n_chunks, z_ref, hc_ref, hp_ref, hs_ref, s0_ref, ones_ref,
                  *rest):
    n_w = len(_MixerWeights._fields)
    W = _MixerWeights(*rest[:n_w])
    mix_ref, newc_ref, newp_ref, news_ref, news_wkv_ref, ubuf, pbuf, qprev, s_scr = rest[n_w:]
    C = CHUNK
    t = pl.program_id(1)

    @pl.when(t == 0)
    def _():
        zero = jnp.zeros((HEAD_SIZE, HEAD_SIZE), F32)
        for i in range(n_seqs):
            ubuf[i, 0:2, :] = jnp.zeros((2, CONV_DIM), F32)
            ubuf[i, 2:32, :] = hc_ref[i]
            pbuf[i, 0:1, :] = jnp.zeros((1, POOL_DIM), F32)
            pbuf[i, 1:16, :] = hp_ref[i]
            qprev[i] = jnp.zeros(qprev.shape[1:], F32)
            qprev[i, 0:1, :] = hs_ref[i]
            for p in range(PAIRS):
                top = jnp.concatenate([s0_ref[i, 2 * p], zero], axis=-1)
                bot = jnp.concatenate([zero, s0_ref[i, 2 * p + 1]], axis=-1)
                s_scr[i, p] = jnp.concatenate([top, bot], axis=0)

    ones_bd = ones_ref[...]
    ids = [(i, c) for c in range(n_chunks) for i in range(n_seqs)]
    seqs = [_Unit(i, c, z_ref, ubuf, pbuf, qprev, s_scr, mix_ref) for i, c in ids]
    prep = [_rwkv_prep_pieces(s, W, ones_bd) for s in seqs]
    side = [_conv_pool_pieces(s, W, start_pos + (t * n_chunks + c) * C)
            for s, (_, c) in zip(seqs, ids)]
    post = [_rwkv_post_pieces(s, W, ones_bd) for s in seqs]
    pins = []
    never = t < 0
    rounds = [range(i, i + UNITS_PER_ROUND) for i in range(0, len(seqs), UNITS_PER_ROUND)]
    gather = lambda pieces, ids: [f for i in ids for f in pieces[i]]
    for f in gather(prep, rounds[0]):
        f()
    for r, ids in enumerate(rounds):
        stages = [st for group in zip(*[_wkv_stages(seqs[i], pins, never) for i in ids])
                  for st in group]
        fillers = gather(side, ids)
        if r + 1 < len(rounds):
            fillers += gather(prep, rounds[r + 1])
        if r:
            fillers += gather(post, rounds[r - 1])
        _run_interleaved(stages, fillers, pins)
    for f in gather(post, rounds[-1]):
        f()

    @pl.when(t == pl.num_programs(1) - 1)
    def _():
        for i in range(n_seqs):
            newc_ref[i] = ubuf[i, 2:32, :]
            newp_ref[i] = pbuf[i, 1:16, :]
            news_ref[i] = qprev[i, 0:1, :]
            for p in range(PAIRS):
                sp = s_scr[i, p]
                news_wkv_ref[i, 2 * p] = sp[0:HEAD_SIZE, 0:HEAD_SIZE]
                news_wkv_ref[i, 2 * p + 1] = sp[HEAD_SIZE:, HEAD_SIZE:]


def _mixers(z, hist_conv, hist_pool, hist_shift, state_wkv, ones_bd, wts, layer, start_pos):
    B, L, _ = z.shape
    n_chunks = MAX_CHUNKS_PER_STEP if L % (MAX_CHUNKS_PER_STEP * CHUNK) == 0 else 1
    C, G = n_chunks * CHUNK, SEQS_PER_STEP
    assert L % C == 0 and B % G == 0

    def per_layer(shape):
        nd = len(shape)
        return pl.BlockSpec((None,) + shape, lambda b, t: (layer,) + (0,) * nd)

    def per_seq(shape):
        nd = len(shape)
        return pl.BlockSpec((None, G) + shape, lambda b, t: (layer, b) + (0,) * nd)

    def out_seq(shape):
        nd = len(shape)
        return pl.BlockSpec((G,) + shape, lambda b, t: (b,) + (0,) * nd)

    vec = lambda n: per_layer((1, n))
    in_specs = [
        pl.BlockSpec((G, C, IN_PROJ), lambda b, t: (b, t, 0)),
        per_seq((CONV_HIST, CONV_DIM)),
        per_seq((POOL_HIST, POOL_DIM)),
        per_seq((1, RWKV_PROJ)),
        per_seq((RWKV_HEADS, HEAD_SIZE, HEAD_SIZE)),
        pl.BlockSpec((LANES, LANES), lambda b, t: (0, 0)),
        per_layer((CONV_WIDTH, CONV_DIM)), vec(CONV_DIM), vec(CONV_DIM), vec(CONV_DIM),
        per_layer((len(POOL_WINDOWS), POOL_GROUP_DIM, POOL_GROUP_DIM)), vec(POOL_DIM),
        vec(RWKV_PROJ), vec(RWKV_DIM), per_layer((LORA, RWKV_DIM)), vec(RWKV_DIM),
        per_layer((LORA, RWKV_DIM)), per_layer((LORA, RWKV_DIM)),
        vec(RWKV_DIM), vec(RWKV_DIM), vec(RWKV_DIM), vec(RWKV_DIM), vec(RWKV_DIM),
    ]
    out_specs = [
        pl.BlockSpec((G, C, D_MODEL), lambda b, t: (b, t, 0)),
        out_seq((CONV_HIST, CONV_DIM)),
        out_seq((POOL_HIST, POOL_DIM)),
        out_seq((1, RWKV_PROJ)),
        out_seq((RWKV_HEADS, HEAD_SIZE, HEAD_SIZE)),
    ]
    out_shape = [
        jax.ShapeDtypeStruct((B, L, D_MODEL), BF16),
        jax.ShapeDtypeStruct((B, CONV_HIST, CONV_DIM), F32),
        jax.ShapeDtypeStruct((B, POOL_HIST, POOL_DIM), F32),
        jax.ShapeDtypeStruct((B, 1, RWKV_PROJ), F32),
        jax.ShapeDtypeStruct((B, RWKV_HEADS, HEAD_SIZE, HEAD_SIZE), F32),
    ]
    scratch = [
        pltpu.VMEM((G, 32 + CHUNK, CONV_DIM), F32),
        pltpu.VMEM((G, 16 + CHUNK, POOL_DIM), F32),
        pltpu.VMEM((G, SUBLANES, RWKV_PROJ), F32),
        pltpu.VMEM((G, PAIRS, LANES, LANES), F32),
    ]
    return pl.pallas_call(
        functools.partial(_mixer_kernel, start_pos, G, n_chunks),
        grid=(B // G, L // C),
        in_specs=in_specs,
        out_specs=out_specs,
        out_shape=out_shape,
        scratch_shapes=scratch,
        compiler_params=pltpu.CompilerParams(
            dimension_semantics=("arbitrary", "arbitrary"), vmem_limit_bytes=VMEM_LIMIT),
    )(z, hist_conv, hist_pool, hist_shift, state_wkv, ones_bd, *wts)


def _prepare_weights(weights):
    weights = list(weights)
    w_in, w_out, ffn_down = weights[1], weights[19], weights[23]
    weights[1] = _transpose_cast_weight(jnp.swapaxes(w_in, 1, 2), IN_PROJ_PAD, 1024)
    weights[19] = _cast_weight(w_out, 512)
    weights[23] = _cast_weight(ffn_down, 512)
    return tuple(weights)


def _trunk(x, hist_conv, hist_pool, hist_shift, state_wkv, start_pos, weights):
    (norm_mix, w_in, conv_w, conv_b, conv_ln_g, conv_ln_b, pool_w, pool_scale,
     shift_mu, decay_w0, decay_up, iclr_a0, iclr_up, gate_up, k_k, k_a, r_k, gn_g, gn_b,
     w_out, norm_ffn, ffn_gate, ffn_up, ffn_down, norm_final) = weights
    B, L, _ = x.shape
    T = B * L
    as_rows = lambda a: a.reshape(DEPTH, 1, a.shape[-1])
    mixer_wts = (conv_w, as_rows(conv_b), as_rows(conv_ln_g), as_rows(conv_ln_b), pool_w,
                 as_rows(pool_scale), as_rows(shift_mu), as_rows(decay_w0), decay_up,
                 as_rows(iclr_a0), iclr_up, gate_up, as_rows(k_k), as_rows(k_a),
                 as_rows(r_k.reshape(DEPTH, RWKV_DIM)), as_rows(gn_g), as_rows(gn_b))
    norm_mix3 = as_rows(norm_mix)
    idx = jnp.arange(LANES) // HEAD_SIZE
    ones_bd = (idx[:, None] == idx[None, :]).astype(BF16)

    x = x.reshape(T, D_MODEL)
    h = x
    convs, pools, shifts, wkvs = [], [], [], []
    for layer in range(DEPTH):
        last = layer == DEPTH - 1
        z = _in_proj(h, norm_mix3, w_in, layer, fuse_norm=layer == 0)
        mix, c_new, p_new, s_new, S_new = _mixers(
            z.reshape(B, L, IN_PROJ), hist_conv, hist_pool, hist_shift, state_wkv, ones_bd,
            mixer_wts, layer, start_pos)
        x, h = _residual_proj(x, mix.reshape(T, D_MODEL), w_out, layer, norm_ffn[layer][None],
                              final=False)
        act = _ffn_up(h, ffn_gate, ffn_up, layer)
        gain = norm_final[None] if last else norm_mix[layer + 1][None]
        out = _residual_proj(x, act, ffn_down, layer, gain, final=last)
        if not last:
            x, h = out
        convs.append(c_new)
        pools.append(p_new)
        shifts.append(s_new)
        wkvs.append(S_new)
    y = out.reshape(B, L, D_MODEL)
    return y, jnp.stack(convs), jnp.stack(pools), jnp.stack(shifts), jnp.stack(wkvs)


def kernel(x_prompt, x_sample, cache_conv, cache_pool, state_shift, state_wkv, norm_mix, w_in,
           conv_w, conv_b, conv_ln_g, conv_ln_b, pool_w, pool_scale, shift_mu, decay_w0, decay_up,
           iclr_a0, iclr_up, gate_up, k_k, k_a, r_k, gn_g, gn_b, w_out, norm_ffn, ffn_gate,
           ffn_up, ffn_down, norm_final):
    weights = (norm_mix, w_in, conv_w, conv_b, conv_ln_g, conv_ln_b, pool_w, pool_scale,
               shift_mu, decay_w0, decay_up, iclr_a0, iclr_up, gate_up, k_k, k_a, r_k, gn_g, gn_b,
               w_out, norm_ffn, ffn_gate, ffn_up, ffn_down, norm_final)
    weights = _prepare_weights(weights)
    bp = x_prompt.shape[0]
    zc = jnp.zeros((DEPTH, bp, CONV_HIST, CONV_DIM), F32)
    zp = jnp.zeros((DEPTH, bp, POOL_HIST, POOL_DIM), F32)
    zs = jnp.zeros((DEPTH, bp, 1, RWKV_PROJ), F32)
    zw = jnp.zeros((DEPTH, bp, RWKV_HEADS, HEAD_SIZE, HEAD_SIZE), F32)
    y_p, p_conv, p_pool, p_shift, p_wkv = _trunk(x_prompt, zc, zp, zs, zw, 0, weights)
    y_s, s_conv, s_pool, s_shift, s_wkv = _trunk(x_sample, cache_conv, cache_pool, state_shift,
                                                 state_wkv, PAST_LEN, weights)
    return (y_p, y_s, p_conv, p_pool, p_shift, p_wkv, s_conv, s_pool, s_shift, s_wkv)
```

```python
import collections
import functools
import math

import jax
import jax.numpy as jnp
from jax import lax
from jax.experimental import pallas as pl
from jax.experimental.pallas import tpu as pltpu

F32 = jnp.float32
BF16 = jnp.bfloat16

D_MODEL = 2048
DEPTH = 4
PAST_LEN = 4096
CONV_DIM = 512
CONV_WIDTH = 31
CONV_HIST = CONV_WIDTH - 1
POOL_DIM = 512
POOL_WINDOWS = (2, 4, 8, 16)
POOL_GROUP_DIM = 128
POOL_HIST = max(POOL_WINDOWS) - 1
RWKV_DIM = 1024
HEAD_SIZE = 64
RWKV_HEADS = RWKV_DIM // HEAD_SIZE
LORA = 64
RWKV_PROJ = 3 * RWKV_DIM + 3 * LORA
IN_PROJ = 2 * CONV_DIM + POOL_DIM + RWKV_PROJ
POOL_OFF = 2 * CONV_DIM
RWKV_OFF = POOL_OFF + POOL_DIM
D_FF = 5632
RMS_EPS = 1e-6
DECAY_SCALE = math.exp(-0.5)
LN_EPS = 1e-5
GN_EPS = 64e-5

LANES = 128
SUBLANES = 8
PAIRS = RWKV_DIM // LANES
SEQS_PER_STEP = 4
MAX_CHUNKS_PER_STEP = 2
UNITS_PER_ROUND = 2
CHUNK = 64
SOLVE_SQUARINGS = 5
VMEM_LIMIT = 56 * 1024 * 1024


def _tiles(n_tokens):
    tm = min(n_tokens, 1024)
    assert n_tokens % tm == 0
    return dict(tm=tm, tn=1024, tn_ff=512, tk_res=512)


def _cast_kernel(w_ref, o_ref):
    o_ref[...] = w_ref[...].astype(BF16)


def _cast_weight(w, rows):
    depth, K, N = w.shape
    return pl.pallas_call(
        _cast_kernel,
        grid=(depth, K // rows),
        in_specs=[pl.BlockSpec((None, rows, N), lambda l, k: (l, k, 0))],
        out_specs=pl.BlockSpec((None, rows, N), lambda l, k: (l, k, 0)),
        out_shape=jax.ShapeDtypeStruct(w.shape, BF16),
        compiler_params=pltpu.CompilerParams(dimension_semantics=("arbitrary", "arbitrary")),
    )(w)


def _bdot(a, b):
    return jnp.dot(a.astype(BF16), b.astype(BF16), preferred_element_type=F32)


def _bdot_nt(a, b):
    return lax.dot_general(a.astype(BF16), b.astype(BF16), (((1,), (1,)), ((), ())),
                           preferred_element_type=F32)


def _sigmoid(x):
    return 1.0 / (1.0 + jnp.exp(-x))


def _rms_norm(x, gain):
    ms = jnp.mean(x * x, axis=-1, keepdims=True)
    return x * lax.rsqrt(ms + RMS_EPS) * gain


def _in_proj_kernel(fuse_norm, x_ref, g_ref, w_ref, o_ref, *scratch):
    if fuse_norm:
        h_ref, = scratch

        @pl.when(pl.program_id(1) == 0)
        def _():
            h_ref[...] = _rms_norm(x_ref[...], g_ref[...]).astype(BF16)

        h = h_ref[...]
    else:
        h = x_ref[...]
    o_ref[...] = lax.dot_general(h, w_ref[...].astype(BF16), (((1,), (1,)), ((), ())),
                                 preferred_element_type=F32)


def _in_proj(x, g, w, layer, fuse_norm):
    T = x.shape[0]
    t = _tiles(T)
    tm, tn = t["tm"], t["tn"]
    return pl.pallas_call(
        functools.partial(_in_proj_kernel, fuse_norm),
        grid=(T // tm, pl.cdiv(IN_PROJ, tn)),
        in_specs=[
            pl.BlockSpec((tm, D_MODEL), lambda m, n: (m, 0)),
            pl.BlockSpec((None, 1, D_MODEL), lambda m, n: (layer, 0, 0)),
            pl.BlockSpec((None, tn, D_MODEL), lambda m, n: (layer, n, 0)),
        ],
        out_specs=pl.BlockSpec((tm, tn), lambda m, n: (m, n)),
        out_shape=jax.ShapeDtypeStruct((T, IN_PROJ), F32),
        scratch_shapes=[pltpu.VMEM((tm, D_MODEL), BF16)] if fuse_norm else [],
        compiler_params=pltpu.CompilerParams(
            dimension_semantics=("arbitrary", "arbitrary"), vmem_limit_bytes=VMEM_LIMIT),
    )(x, g, w)


def _ffn_up_kernel(h_ref, wg_ref, wu_ref, o_ref):
    h = h_ref[...]
    gate = jnp.dot(h, wg_ref[...].astype(BF16), preferred_element_type=F32)
    up = jnp.dot(h, wu_ref[...].astype(BF16), preferred_element_type=F32)
    o_ref[...] = (gate * _sigmoid(gate) * up).astype(BF16)


def _ffn_up(h, wg, wu, layer):
    T = h.shape[0]
    t = _tiles(T)
    tm, tn = t["tm"], t["tn_ff"]
    return pl.pallas_call(
        _ffn_up_kernel,
        grid=(T // tm, D_FF // tn),
        in_specs=[
            pl.BlockSpec((tm, D_MODEL), lambda m, n: (m, 0)),
            pl.BlockSpec((None, D_MODEL, tn), lambda m, n: (layer, 0, n)),
            pl.BlockSpec((None, D_MODEL, tn), lambda m, n: (layer, 0, n)),
        ],
        out_specs=pl.BlockSpec((tm, tn), lambda m, n: (m, n)),
        out_shape=jax.ShapeDtypeStruct((T, D_FF), BF16),
        compiler_params=pltpu.CompilerParams(
            dimension_semantics=("arbitrary", "arbitrary"), vmem_limit_bytes=VMEM_LIMIT),
    )(h, wg, wu)


def _residual_proj_kernel(final, x_ref, a_ref, w_ref, g_ref, o_ref, *h_ref):
    k = pl.program_id(1)

    @pl.when(k == 0)
    def _():
        o_ref[...] = x_ref[...]

    o_ref[...] += jnp.dot(a_ref[...], w_ref[...], preferred_element_type=F32)

    @pl.when(k == pl.num_programs(1) - 1)
    def _():
        normed = _rms_norm(o_ref[...], g_ref[...])
        if final:
            o_ref[...] = normed
        else:
            h_ref[0][...] = normed.astype(BF16)


def _residual_proj(x, a, w, layer, gain, final):
    T, K = a.shape
    t = _tiles(T)
    tm, tk = t["tm"], t["tk_res"]
    row_spec = pl.BlockSpec((tm, D_MODEL), lambda m, k: (m, 0))
    out_shape = [jax.ShapeDtypeStruct((T, D_MODEL), F32)]
    if not final:
        out_shape.append(jax.ShapeDtypeStruct((T, D_MODEL), BF16))
    out = pl.pallas_call(
        functools.partial(_residual_proj_kernel, final),
        grid=(T // tm, K // tk),
        in_specs=[
            row_spec,
            pl.BlockSpec((tm, tk), lambda m, k: (m, k)),
            pl.BlockSpec((None, tk, D_MODEL), lambda m, k: (layer, k, 0)),
            pl.BlockSpec((1, D_MODEL), lambda m, k: (0, 0)),
        ],
        out_specs=[row_spec] * len(out_shape),
        out_shape=out_shape,
        compiler_params=pltpu.CompilerParams(
            dimension_semantics=("arbitrary", "arbitrary"), vmem_limit_bytes=VMEM_LIMIT),
    )(x, a, w, gain)
    return out[0] if final else out


class _Unit:
    def __init__(self, i, c, z_ref, ubuf, pbuf, qprev, s_scr, mix_ref):
        rows = pl.ds(c * CHUNK, CHUNK)
        self.z = z_ref.at[i, rows]
        self.ubuf = ubuf.at[i]
        self.pbuf = pbuf.at[i]
        self.qprev = qprev.at[i]
        self.s = s_scr.at[i]
        self.mix = mix_ref.at[i, rows]
        self.v = {}


def _head_sums(x, ones_bd):
    stacked = jnp.concatenate([x[:, p * LANES:(p + 1) * LANES] for p in range(PAIRS)], axis=0)
    s = _bdot(stacked, ones_bd)
    C = x.shape[0]
    return jnp.concatenate([s[p * C:(p + 1) * C, :] for p in range(PAIRS)], axis=-1)


def _lane_tile_sum(x):
    tiles = [x[:, i * LANES:(i + 1) * LANES] for i in range(x.shape[-1] // LANES)]
    while len(tiles) > 1:
        tiles = [a + b for a, b in zip(tiles[0::2], tiles[1::2])] + tiles[len(tiles) & ~1:]
    return tiles[0]


def _conv_pool_pieces(s, W, pos0):
    C = CHUNK
    v = s.v

    def glu():
        val = s.z[:, 0:CONV_DIM]
        gate = s.z[:, CONV_DIM:2 * CONV_DIM]
        s.ubuf[32:32 + C, :] = val * _sigmoid(gate)

    def taps(tile):
        def f():
            lanes = slice(tile * LANES, (tile + 1) * LANES)
            rows = 32 + C
            full = s.ubuf[:, lanes]
            shifted = [full] + [pltpu.roll(full, rows - k, 0) for k in range(1, SUBLANES)]
            acc = jnp.zeros((C, LANES), F32) + W.conv_b[:, lanes]
            for j in range(CONV_WIDTH):
                phase, base = (2 + j) % SUBLANES, (2 + j) // SUBLANES * SUBLANES
                acc = acc + shifted[phase][base:base + C, :] * W.conv_w[j:j + 1, lanes]
            v["conv%d" % tile] = acc
            return acc
        return f

    def norm():
        acc = jnp.concatenate([v.pop("conv%d" % t) for t in range(CONV_DIM // LANES)], axis=-1)
        mu = jnp.mean(acc, axis=-1, keepdims=True)
        cen = acc - mu
        var = jnp.mean(cen * cen, axis=-1, keepdims=True)
        hn = cen * lax.rsqrt(var + LN_EPS) * W.ln_g[...] + W.ln_b[...]
        out = hn * _sigmoid(hn)
        s.mix[:, 0:CONV_DIM] = out.astype(BF16)
        s.ubuf[2:32, :] = s.ubuf[C + 2:C + 32, :]
        return _lane_tile_sum(out)

    def pool():
        s.pbuf[16:16 + C, :] = s.z[:, POOL_OFF:POOL_OFF + POOL_DIM]
        pos = pos0 + lax.broadcasted_iota(jnp.int32, (C, POOL_GROUP_DIM), 0)
        outs = []
        for gi, w in enumerate(POOL_WINDOWS):
            lo, hi = gi * POOL_GROUP_DIM, (gi + 1) * POOL_GROUP_DIM
            tok = s.pbuf[16:16 + C, lo:hi]
            tot = tok
            for j in range(1, w):
                tot = tot + s.pbuf[16 - j:16 - j + C, lo:hi]
            cnt = jnp.minimum(w, pos + 1).astype(F32)
            d = tot / cnt - tok
            outs.append(_bdot(d, W.pool_w[gi]))
        out = jnp.concatenate(outs, axis=-1) * W.pool_scale[...]
        s.mix[:, CONV_DIM:CONV_DIM + POOL_DIM] = out.astype(BF16)
        s.pbuf[1:16, :] = s.pbuf[C + 1:C + 16, :]
        return _lane_tile_sum(out)

    return [glu] + [taps(t) for t in range(CONV_DIM // LANES)] + [norm, pool]


def _rwkv_prep_pieces(s, W, ones_bd):
    C = CHUNK
    v = s.v

    def shift(name, off, width):
        def f():
            q = s.z[:, RWKV_OFF + off:RWKV_OFF + off + width]
            rolled = pltpu.roll(q, 1, 0)
            first_row = lax.broadcasted_iota(jnp.int32, (C, width), 0) == 0
            q_prev = jnp.where(first_row, s.qprev[0:1, off:off + width], rolled)
            v[name] = q + (q_prev - q) * W.mu[:, off:off + width]
            s.qprev[0:1, off:off + width] = q[C - 1:C, :]
        return f

    def lora():
        lo = v.pop("lo")
        w_lo, a_lo, g_lo = lo[:, 0:LORA], lo[:, LORA:2 * LORA], lo[:, 2 * LORA:3 * LORA]
        x = W.w0[...] + _bdot(jnp.tanh(w_lo), W.w_up[...])
        v["lw"] = -DECAY_SCALE * _sigmoid(x)
        v["a"] = _sigmoid(W.a0[...] + _bdot(a_lo, W.a_up[...]))
        v["g"] = _bdot(_sigmoid(g_lo), W.g_up[...])

    def keys():
        k, a = v.pop("k"), v["a"]
        kk = k * W.k_k[...]
        kk = kk * lax.rsqrt(jnp.maximum(_head_sums(kk * kk, ones_bd), 1e-24))
        v["kk"] = kk
        v["k2"] = k * (1.0 + (a - 1.0) * W.k_a[...])

    def decay():
        lw = v.pop("lw")
        row = lax.broadcasted_iota(jnp.int32, (C, C), 0)
        col = lax.broadcasted_iota(jnp.int32, (C, C), 1)
        tri = (col <= row).astype(BF16)
        lw_hi = lw.astype(BF16)
        rem = lw - lw_hi.astype(F32)
        lw_mid = rem.astype(BF16)
        lw_lo = (rem - lw_mid.astype(F32)).astype(BF16)
        cum = (jnp.dot(tri, lw_hi, preferred_element_type=F32)
               + jnp.dot(tri, lw_mid, preferred_element_type=F32)
               + jnp.dot(tri, lw_lo, preferred_element_type=F32))
        tot = cum[C - 1:C, :]
        v["e_ex"] = jnp.exp(cum - lw)
        v["e_in"] = jnp.exp(cum)
        v["e_inv"] = jnp.exp(-cum)
        v["e_end"] = jnp.exp(tot - cum)
        v["g_tot"] = jnp.exp(tot)

    def products():
        kk, k2, a = v.pop("kk"), v["k2"], v.pop("a")
        b = kk * a
        e_inv, e_end = v.pop("e_inv"), v.pop("e_end")
        v["A_t"] = -kk * v.pop("e_ex")
        v["R_t"] = v["r"] * v.pop("e_in")
        v["B_t"] = b * e_inv
        v["K_t"] = k2 * e_inv
        v["B_e"] = b * e_end
        v["K_e"] = k2 * e_end

    return [shift("lo", 3 * RWKV_DIM, 3 * LORA), lora, shift("k", RWKV_DIM, RWKV_DIM), keys, decay,
            shift("r", 0, RWKV_DIM), shift("v", 2 * RWKV_DIM, RWKV_DIM), products]


def _wkv_stages(s, pins, never):
    C = CHUNK
    v = s.v
    pairs = range(PAIRS)
    sl = [slice(p * LANES, (p + 1) * LANES) for p in pairs]
    low2 = lax.broadcasted_iota(jnp.int32, (2 * C, LANES), 1) < HEAD_SIZE
    low1 = lax.broadcasted_iota(jnp.int32, (C, LANES), 1) < HEAD_SIZE
    r4 = lax.broadcasted_iota(jnp.int32, (C, 4 * C), 0)
    c4 = lax.broadcasted_iota(jnp.int32, (C, 4 * C), 1) & (C - 1)
    strict = c4 < r4
    incl = c4 <= r4
    rr = lax.broadcasted_iota(jnp.int32, (LANES, LANES), 0)
    cc = lax.broadcasted_iota(jnp.int32, (LANES, LANES), 1)
    same_head = (rr < HEAD_SIZE) == (cc < HEAD_SIZE)
    z_tile = jnp.zeros((C, LANES), BF16)
    z_rows = jnp.zeros((C, 2 * LANES), BF16)
    w = {}

    def masked_rows(p):
        u = w["U"][p]
        return jnp.concatenate([jnp.where(low1, u, 0.0).astype(BF16), w["v_low"][p],
                                jnp.where(low1, 0.0, u).astype(BF16), w["v_high"][p]], axis=0)

    def gram():
        w["S0"] = [s.s[p] for p in pairs]
        w["v_low"] = [jnp.where(low1, v["v"][:, sl[p]], 0.0).astype(BF16) for p in pairs]
        w["v_high"] = [jnp.where(low1, 0.0, v["v"][:, sl[p]]).astype(BF16) for p in pairs]
        A_t, R_t, B_t, K_t = v.pop("A_t"), v.pop("R_t"), v.pop("B_t"), v.pop("K_t")
        GG = []
        for p in pairs:
            AR = jnp.concatenate([A_t[:, sl[p]], R_t[:, sl[p]]], axis=0)
            lhs = jnp.concatenate([jnp.where(low2, AR, 0.0), jnp.where(low2, 0.0, AR)], axis=0)
            rhs = jnp.concatenate([B_t[:, sl[p]], K_t[:, sl[p]], w["S0"][p]], axis=0)
            GG.append(_bdot_nt(lhs, rhs))
        side = lambda g, r0: jnp.concatenate(
            [g[r0:r0 + C, 0:2 * C], g[r0 + 2 * C:r0 + 3 * C, 0:2 * C]], axis=-1)
        w["P"] = [jnp.where(strict, side(g, 0), 0.0).astype(BF16) for g in GG]
        w["M"] = [jnp.where(incl, side(g, C), 0.0).astype(BF16) for g in GG]
        w["U"] = [g[0:C, 2 * C:] + g[2 * C:3 * C, 2 * C:] for g in GG]
        w["AS_bot"] = [g[C:2 * C, 2 * C:] + g[3 * C:4 * C, 2 * C:] for g in GG]

    def take_pins():
        while pins:
            w["U"][PAIRS - 1] = jnp.where(never, pins.pop(), w["U"][PAIRS - 1])

    def level(square):
        def f():
            take_pins()
            Z = [masked_rows(p) for p in pairs]
            P = w["P"]
            w["U"] = [w["U"][p] + jnp.dot(P[p], Z[p], preferred_element_type=F32) for p in pairs]
            if square:
                sq = [jnp.concatenate([
                    jnp.concatenate([P[p][:, 0:2 * C], z_tile], axis=-1), z_rows,
                    jnp.concatenate([z_tile, P[p][:, 2 * C:]], axis=-1), z_rows], axis=0)
                    for p in pairs]
                w["P"] = [jnp.dot(P[p], sq[p], preferred_element_type=F32).astype(BF16)
                          for p in pairs]
        return f

    def finish():
        take_pins()
        B_e, K_e, g_tot = v.pop("B_e"), v.pop("K_e"), v.pop("g_tot")
        ys = []
        for p in pairs:
            ys.append(w["AS_bot"][p]
                      + jnp.dot(w["M"][p], masked_rows(p), preferred_element_type=F32))
            UV = jnp.concatenate([w["U"][p], v["v"][:, sl[p]]], axis=0)
            BKe = jnp.concatenate([B_e[:, sl[p]], K_e[:, sl[p]]], axis=0)
            upd = _bdot(UV.T, BKe)
            s.s[p] = w["S0"][p] * g_tot[:, sl[p]] + jnp.where(same_head, upd, 0.0)
        v["y"] = jnp.concatenate(ys, axis=-1)
        w.clear()

    return [gram] + [level(True) for _ in range(SOLVE_SQUARINGS)] + [level(False), finish]


def _rwkv_post_pieces(s, W, ones_bd):
    v = s.v
    inv_n = 1.0 / HEAD_SIZE

    def center():
        y = v.pop("y")
        v["yc"] = y - _head_sums(y, ones_bd) * inv_n

    def scale():
        yc = v.pop("yc")
        yv = _head_sums(yc * yc, ones_bd) * inv_n
        v["yn"] = yc * lax.rsqrt(yv + GN_EPS) * W.gn_g[...] + W.gn_b[...]

    def bonus():
        v["bonus"] = _head_sums(v.pop("r") * v.pop("k2") * W.r_k[...], ones_bd) * v.pop("v")

    def gate():
        out = (v.pop("yn") + v.pop("bonus")) * v.pop("g")
        s.mix[:, CONV_DIM + POOL_DIM:] = out.astype(BF16)
        return _lane_tile_sum(out)

    return [center, scale, bonus, gate]


def _run_interleaved(stages, fillers, pins):
    n = len(stages)
    for i, stage in enumerate(stages):
        stage()
        for f in fillers[i * len(fillers) // n:(i + 1) * len(fillers) // n]:
            token = f()
            if token is not None:
                pins.append(token)


_MixerWeights = collections.namedtuple(
    "_MixerWeights", "conv_w conv_b ln_g ln_b pool_w pool_scale mu w0 w_up a0 a_up g_up k_k k_a "
                     "r_k gn_g gn_b")


def _mixer_kernel(start_pos, n_seqs, n_chunks, z_ref, hc_ref, hp_ref, hs_ref, s0_ref, ones_ref,
                  *rest):
    n_w = len(_MixerWeights._fields)
    W = _MixerWeights(*rest[:n_w])
    mix_ref, newc_ref, newp_ref, news_ref, news_wkv_ref, ubuf, pbuf, qprev, s_scr = rest[n_w:]
    C = CHUNK
    t = pl.program_id(1)

    @pl.when(t == 0)
    def _():
        zero = jnp.zeros((HEAD_SIZE, HEAD_SIZE), F32)
        for i in range(n_seqs):
            ubuf[i, 0:2, :] = jnp.zeros((2, CONV_DIM), F32)
            ubuf[i, 2:32, :] = hc_ref[i]
            pbuf[i, 0:1, :] = jnp.zeros((1, POOL_DIM), F32)
            pbuf[i, 1:16, :] = hp_ref[i]
            qprev[i] = jnp.zeros(qprev.shape[1:], F32)
            qprev[i, 0:1, :] = hs_ref[i]
            for p in range(PAIRS):
                top = jnp.concatenate([s0_ref[i, 2 * p], zero], axis=-1)
                bot = jnp.concatenate([zero, s0_ref[i, 2 * p + 1]], axis=-1)
                s_scr[i, p] = jnp.concatenate([top, bot], axis=0)

    ones_bd = ones_ref[...]
    ids = [(i, c) for c in range(n_chunks) for i in range(n_seqs)]
    seqs = [_Unit(i, c, z_ref, ubuf, pbuf, qprev, s_scr, mix_ref) for i, c in ids]
    prep = [_rwkv_prep_pieces(s, W, ones_bd) for s in seqs]
    side = [_conv_pool_pieces(s, W, start_pos + (t * n_chunks + c) * C)
            for s, (_, c) in zip(seqs, ids)]
    post = [_rwkv_post_pieces(s, W, ones_bd) for s in seqs]
    pins = []
    never = t < 0
    rounds = [range(i, i + UNITS_PER_ROUND) for i in range(0, len(seqs), UNITS_PER_ROUND)]
    gather = lambda pieces, ids: [f for i in ids for f in pieces[i]]
    for f in gather(prep, rounds[0]):
        f()
    for r, ids in enumerate(rounds):
        stages = [st for group in zip(*[_wkv_stages(seqs[i], pins, never) for i in ids])
                  for st in group]
        fillers = gather(side, ids)
        if r + 1 < len(rounds):
            fillers += gather(prep, rounds[r + 1])
        if r:
            fillers += gather(post, rounds[r - 1])
        _run_interleaved(stages, fillers, pins)
    for f in gather(post, rounds[-1]):
        f()

    @pl.when(t == pl.num_programs(1) - 1)
    def _():
        for i in range(n_seqs):
            newc_ref[i] = ubuf[i, 2:32, :]
            newp_ref[i] = pbuf[i, 1:16, :]
            news_ref[i] = qprev[i, 0:1, :]
            for p in range(PAIRS):
                sp = s_scr[i, p]
                news_wkv_ref[i, 2 * p] = sp[0:HEAD_SIZE, 0:HEAD_SIZE]
                news_wkv_ref[i, 2 * p + 1] = sp[HEAD_SIZE:, HEAD_SIZE:]


def _mixers(z, hist_conv, hist_pool, hist_shift, state_wkv, ones_bd, wts, layer, start_pos):
    B, L, _ = z.shape
    n_chunks = MAX_CHUNKS_PER_STEP if L % (MAX_CHUNKS_PER_STEP * CHUNK) == 0 else 1
    C, G = n_chunks * CHUNK, SEQS_PER_STEP
    assert L % C == 0 and B % G == 0

    def per_layer(shape):
        nd = len(shape)
        return pl.BlockSpec((None,) + shape, lambda b, t: (layer,) + (0,) * nd)

    def per_seq(shape):
        nd = len(shape)
        return pl.BlockSpec((None, G) + shape, lambda b, t: (layer, b) + (0,) * nd)

    def out_seq(shape):
        nd = len(shape)
        return pl.BlockSpec((G,) + shape, lambda b, t: (b,) + (0,) * nd)

    vec = lambda n: per_layer((1, n))
    in_specs = [
        pl.BlockSpec((G, C, IN_PROJ), lambda b, t: (b, t, 0)),
        per_seq((CONV_HIST, CONV_DIM)),
        per_seq((POOL_HIST, POOL_DIM)),
        per_seq((1, RWKV_PROJ)),
        per_seq((RWKV_HEADS, HEAD_SIZE, HEAD_SIZE)),
        pl.BlockSpec((LANES, LANES), lambda b, t: (0, 0)),
        per_layer((CONV_WIDTH, CONV_DIM)), vec(CONV_DIM), vec(CONV_DIM), vec(CONV_DIM),
        per_layer((len(POOL_WINDOWS), POOL_GROUP_DIM, POOL_GROUP_DIM)), vec(POOL_DIM),
        vec(RWKV_PROJ), vec(RWKV_DIM), per_layer((LORA, RWKV_DIM)), vec(RWKV_DIM),
        per_layer((LORA, RWKV_DIM)), per_layer((LORA, RWKV_DIM)),
        vec(RWKV_DIM), vec(RWKV_DIM), vec(RWKV_DIM), vec(RWKV_DIM), vec(RWKV_DIM),
    ]
    out_specs = [
        pl.BlockSpec((G, C, D_MODEL), lambda b, t: (b, t, 0)),
        out_seq((CONV_HIST, CONV_DIM)),
        out_seq((POOL_HIST, POOL_DIM)),
        out_seq((1, RWKV_PROJ)),
        out_seq((RWKV_HEADS, HEAD_SIZE, HEAD_SIZE)),
    ]
    out_shape = [
        jax.ShapeDtypeStruct((B, L, D_MODEL), BF16),
        jax.ShapeDtypeStruct((B, CONV_HIST, CONV_DIM), F32),
        jax.ShapeDtypeStruct((B, POOL_HIST, POOL_DIM), F32),
        jax.ShapeDtypeStruct((B, 1, RWKV_PROJ), F32),
        jax.ShapeDtypeStruct((B, RWKV_HEADS, HEAD_SIZE, HEAD_SIZE), F32),
    ]
    scratch = [
        pltpu.VMEM((G, 32 + CHUNK, CONV_DIM), F32),
        pltpu.VMEM((G, 16 + CHUNK, POOL_DIM), F32),
        pltpu.VMEM((G, SUBLANES, RWKV_PROJ), F32),
        pltpu.VMEM((G, PAIRS, LANES, LANES), F32),
    ]
    return pl.pallas_call(
        functools.partial(_mixer_kernel, start_pos, G, n_chunks),
        grid=(B // G, L // C),
        in_specs=in_specs,
        out_specs=out_specs,
        out_shape=out_shape,
        scratch_shapes=scratch,
        compiler_params=pltpu.CompilerParams(
            dimension_semantics=("arbitrary", "arbitrary"), vmem_limit_bytes=VMEM_LIMIT),
    )(z, hist_conv, hist_pool, hist_shift, state_wkv, ones_bd, *wts)


def _prepare_weights(weights):
    weights = list(weights)
    w_in, w_out, ffn_down = weights[1], weights[19], weights[23]
    weights[1] = jnp.swapaxes(w_in, 1, 2)
    weights[19] = _cast_weight(w_out, 512)
    weights[23] = _cast_weight(ffn_down, 512)
    return tuple(weights)


def _trunk(x, hist_conv, hist_pool, hist_shift, state_wkv, start_pos, weights):
    (norm_mix, w_in, conv_w, conv_b, conv_ln_g, conv_ln_b, pool_w, pool_scale,
     shift_mu, decay_w0, decay_up, iclr_a0, iclr_up, gate_up, k_k, k_a, r_k, gn_g, gn_b,
     w_out, norm_ffn, ffn_gate, ffn_up, ffn_down, norm_final) = weights
    B, L, _ = x.shape
    T = B * L
    as_rows = lambda a: a.reshape(DEPTH, 1, a.shape[-1])
    mixer_wts = (conv_w, as_rows(conv_b), as_rows(conv_ln_g), as_rows(conv_ln_b), pool_w,
                 as_rows(pool_scale), as_rows(shift_mu), as_rows(decay_w0), decay_up,
                 as_rows(iclr_a0), iclr_up, gate_up, as_rows(k_k), as_rows(k_a),
                 as_rows(r_k.reshape(DEPTH, RWKV_DIM)), as_rows(gn_g), as_rows(gn_b))
    norm_mix3 = as_rows(norm_mix)
    idx = jnp.arange(LANES) // HEAD_SIZE
    ones_bd = (idx[:, None] == idx[None, :]).astype(BF16)

    x = x.reshape(T, D_MODEL)
    h = x
    convs, pools, shifts, wkvs = [], [], [], []
    for layer in range(DEPTH):
        last = layer == DEPTH - 1
        z = _in_proj(h, norm_mix3, w_in, layer, fuse_norm=layer == 0)
        mix, c_new, p_new, s_new, S_new = _mixers(
            z.reshape(B, L, IN_PROJ), hist_conv, hist_pool, hist_shift, state_wkv, ones_bd,
            mixer_wts, layer, start_pos)
        x, h = _residual_proj(x, mix.reshape(T, D_MODEL), w_out, layer, norm_ffn[layer][None],
                              final=False)
        act = _ffn_up(h, ffn_gate, ffn_up, layer)
        gain = norm_final[None] if last else norm_mix[layer + 1][None]
        out = _residual_proj(x, act, ffn_down, layer, gain, final=last)
        if not last:
            x, h = out
        convs.append(c_new)
        pools.append(p_new)
        shifts.append(s_new)
        wkvs.append(S_new)
    y = out.reshape(B, L, D_MODEL)
    return y, jnp.stack(convs), jnp.stack(pools), jnp.stack(shifts), jnp.stack(wkvs)


def kernel(x_prompt, x_sample, cache_conv, cache_pool, state_shift, state_wkv, norm_mix, w_in,
           conv_w, conv_b, conv_ln_g, conv_ln_b, pool_w, pool_scale, shift_mu, decay_w0, decay_up,
           iclr_a0, iclr_up, gate_up, k_k, k_a, r_k, gn_g, gn_b, w_out, norm_ffn, ffn_gate,
           ffn_up, ffn_down, norm_final):
    weights = (norm_mix, w_in, conv_w, conv_b, conv_ln_g, conv_ln_b, pool_w, pool_scale,
               shift_mu, decay_w0, decay_up, iclr_a0, iclr_up, gate_up, k_k, k_a, r_k, gn_g, gn_b,
               w_out, norm_ffn, ffn_gate, ffn_up, ffn_down, norm_final)
    weights = _prepare_weights(weights)
    bp = x_prompt.shape[0]
    zc = jnp.zeros((DEPTH, bp, CONV_HIST, CONV_DIM), F32)
    zp = jnp.zeros((DEPTH, bp, POOL_HIST, POOL_DIM), F32)
    zs = jnp.zeros((DEPTH, bp, 1, RWKV_PROJ), F32)
    zw = jnp.zeros((DEPTH, bp, RWKV_HEADS, HEAD_SIZE, HEAD_SIZE), F32)
    y_p, p_conv, p_pool, p_shift, p_wkv = _trunk(x_prompt, zc, zp, zs, zw, 0, weights)
    y_s, s_conv, s_pool, s_shift, s_wkv = _trunk(x_sample, cache_conv, cache_pool, state_shift,
                                                 state_wkv, PAST_LEN, weights)
    return (y_p, y_s, p_conv, p_pool, p_shift, p_wkv, s_conv, s_pool, s_shift, s_wkv)
```

```python
import collections
import functools
import math

import jax
import jax.numpy as jnp
from jax import lax
from jax.experimental import pallas as pl
from jax.experimental.pallas import tpu as pltpu

F32 = jnp.float32
BF16 = jnp.bfloat16

D_MODEL = 2048
DEPTH = 4
PAST_LEN = 4096
CONV_DIM = 512
CONV_WIDTH = 31
CONV_HIST = CONV_WIDTH - 1
POOL_DIM = 512
POOL_WINDOWS = (2, 4, 8, 16)
POOL_GROUP_DIM = 128
POOL_HIST = max(POOL_WINDOWS) - 1
RWKV_DIM = 1024
HEAD_SIZE = 64
RWKV_HEADS = RWKV_DIM // HEAD_SIZE
LORA = 64
RWKV_PROJ = 3 * RWKV_DIM + 3 * LORA
IN_PROJ = 2 * CONV_DIM + POOL_DIM + RWKV_PROJ
POOL_OFF = 2 * CONV_DIM
RWKV_OFF = POOL_OFF + POOL_DIM
D_FF = 5632
RMS_EPS = 1e-6
DECAY_SCALE = math.exp(-0.5)
LN_EPS = 1e-5
GN_EPS = 64e-5

LANES = 128
SUBLANES = 8
PAIRS = RWKV_DIM // LANES
SEQS_PER_STEP = 4
MAX_CHUNKS_PER_STEP = 2
UNITS_PER_ROUND = 2
CHUNK = 64
SOLVE_SQUARINGS = 5
VMEM_LIMIT = 56 * 1024 * 1024


def _tiles(n_tokens):
    tm = min(n_tokens, 1024)
    assert n_tokens % tm == 0
    return dict(tm=tm, tn=1024, tn_ff=512, tk_res=512)


def _cast_kernel(w_ref, o_ref):
    o_ref[...] = w_ref[...].astype(BF16)


def _cast_weight(w, rows):
    depth, K, N = w.shape
    return pl.pallas_call(
        _cast_kernel,
        grid=(depth, K // rows),
        in_specs=[pl.BlockSpec((None, rows, N), lambda l, k: (l, k, 0))],
        out_specs=pl.BlockSpec((None, rows, N), lambda l, k: (l, k, 0)),
        out_shape=jax.ShapeDtypeStruct(w.shape, BF16),
        compiler_params=pltpu.CompilerParams(dimension_semantics=("arbitrary", "arbitrary")),
    )(w)


def _bdot(a, b):
    return jnp.dot(a.astype(BF16), b.astype(BF16), preferred_element_type=F32)


def _bdot_nt(a, b):
    return lax.dot_general(a.astype(BF16), b.astype(BF16), (((1,), (1,)), ((), ())),
                           preferred_element_type=F32)


def _sigmoid(x):
    return 1.0 / (1.0 + jnp.exp(-x))


def _rms_norm(x, gain):
    ms = jnp.mean(x * x, axis=-1, keepdims=True)
    return x * lax.rsqrt(ms + RMS_EPS) * gain


def _in_proj_kernel(fuse_norm, x_ref, g_ref, w_ref, o_ref, *scratch):
    if fuse_norm:
        h_ref, = scratch

        @pl.when(pl.program_id(1) == 0)
        def _():
            h_ref[...] = _rms_norm(x_ref[...], g_ref[...]).astype(BF16)

        h = h_ref[...]
    else:
        h = x_ref[...]
    o_ref[...] = lax.dot_general(h, w_ref[...].astype(BF16), (((1,), (1,)), ((), ())),
                                 preferred_element_type=F32)


def _in_proj(x, g, w, layer, fuse_norm):
    T = x.shape[0]
    t = _tiles(T)
    tm, tn = t["tm"], t["tn"]
    return pl.pallas_call(
        functools.partial(_in_proj_kernel, fuse_norm),
        grid=(T // tm, pl.cdiv(IN_PROJ, tn)),
        in_specs=[
            pl.BlockSpec((tm, D_MODEL), lambda m, n: (m, 0)),
            pl.BlockSpec((None, 1, D_MODEL), lambda m, n: (layer, 0, 0)),
            pl.BlockSpec((None, tn, D_MODEL), lambda m, n: (layer, n, 0)),
        ],
        out_specs=pl.BlockSpec((tm, tn), lambda m, n: (m, n)),
        out_shape=jax.ShapeDtypeStruct((T, IN_PROJ), F32),
        scratch_shapes=[pltpu.VMEM((tm, D_MODEL), BF16)] if fuse_norm else [],
        compiler_params=pltpu.CompilerParams(
            dimension_semantics=("arbitrary", "arbitrary"), vmem_limit_bytes=VMEM_LIMIT),
    )(x, g, w)


def _ffn_up_kernel(h_ref, wg_ref, wu_ref, o_ref):
    h = h_ref[...]
    gate = jnp.dot(h, wg_ref[...].astype(BF16), preferred_element_type=F32)
    up = jnp.dot(h, wu_ref[...].astype(BF16), preferred_element_type=F32)
    o_ref[...] = (gate * _sigmoid(gate) * up).astype(BF16)


def _ffn_up(h, wg, wu, layer):
    T = h.shape[0]
    t = _tiles(T)
    tm, tn = t["tm"], t["tn_ff"]
    return pl.pallas_call(
        _ffn_up_kernel,
        grid=(T // tm, D_FF // tn),
        in_specs=[
            pl.BlockSpec((tm, D_MODEL), lambda m, n: (m, 0)),
            pl.BlockSpec((None, D_MODEL, tn), lambda m, n: (layer, 0, n)),
            pl.BlockSpec((None, D_MODEL, tn), lambda m, n: (layer, 0, n)),
        ],
        out_specs=pl.BlockSpec((tm, tn), lambda m, n: (m, n)),
        out_shape=jax.ShapeDtypeStruct((T, D_FF), BF16),
        compiler_params=pltpu.CompilerParams(
            dimension_semantics=("arbitrary", "arbitrary"), vmem_limit_bytes=VMEM_LIMIT),
    )(h, wg, wu)


def _residual_proj_kernel(final, x_ref, a_ref, w_ref, g_ref, o_ref, *h_ref):
    k = pl.program_id(1)
    last = pl.num_programs(1) - 1

    @pl.when(k == 0)
    def _():
        o_ref[...] = x_ref[...]

    @pl.when(k != last)
    def _():
        o_ref[...] += jnp.dot(a_ref[...], w_ref[...], preferred_element_type=F32)

    @pl.when(k == last)
    def _():
        half = o_ref.shape[0] // 2
        for rows in (slice(0, half), slice(half, 2 * half)):
            acc = o_ref[rows, :] + jnp.dot(a_ref[rows, :], w_ref[...], preferred_element_type=F32)
            normed = _rms_norm(acc, g_ref[...])
            if final:
                o_ref[rows, :] = normed
            else:
                o_ref[rows, :] = acc
                h_ref[0][rows, :] = normed.astype(BF16)


def _residual_proj(x, a, w, layer, gain, final):
    T, K = a.shape
    t = _tiles(T)
    tm, tk = t["tm"], t["tk_res"]
    row_spec = pl.BlockSpec((tm, D_MODEL), lambda m, k: (m, 0))
    out_shape = [jax.ShapeDtypeStruct((T, D_MODEL), F32)]
    if not final:
        out_shape.append(jax.ShapeDtypeStruct((T, D_MODEL), BF16))
    out = pl.pallas_call(
        functools.partial(_residual_proj_kernel, final),
        grid=(T // tm, K // tk),
        in_specs=[
            row_spec,
            pl.BlockSpec((tm, tk), lambda m, k: (m, k)),
            pl.BlockSpec((None, tk, D_MODEL), lambda m, k: (layer, k, 0)),
            pl.BlockSpec((1, D_MODEL), lambda m, k: (0, 0)),
        ],
        out_specs=[row_spec] * len(out_shape),
        out_shape=out_shape,
        compiler_params=pltpu.CompilerParams(
            dimension_semantics=("arbitrary", "arbitrary"), vmem_limit_bytes=VMEM_LIMIT),
    )(x, a, w, gain)
    return out[0] if final else out


class _Unit:
    def __init__(self, i, c, z_ref, ubuf, pbuf, qprev, s_scr, mix_ref):
        rows = pl.ds(c * CHUNK, CHUNK)
        self.z = z_ref.at[i, rows]
        self.ubuf = ubuf.at[i]
        self.pbuf = pbuf.at[i]
        self.qprev = qprev.at[i]
        self.s = s_scr.at[i]
        self.mix = mix_ref.at[i, rows]
        self.v = {}


def _head_sums(x, ones_bd):
    stacked = jnp.concatenate([x[:, p * LANES:(p + 1) * LANES] for p in range(PAIRS)], axis=0)
    s = _bdot(stacked, ones_bd)
    C = x.shape[0]
    return jnp.concatenate([s[p * C:(p + 1) * C, :] for p in range(PAIRS)], axis=-1)


def _lane_tile_sum(x):
    tiles = [x[:, i * LANES:(i + 1) * LANES] for i in range(x.shape[-1] // LANES)]
    while len(tiles) > 1:
        tiles = [a + b for a, b in zip(tiles[0::2], tiles[1::2])] + tiles[len(tiles) & ~1:]
    return tiles[0]


def _conv_pool_pieces(s, W, pos0):
    C = CHUNK
    v = s.v

    def glu():
        val = s.z[:, 0:CONV_DIM]
        gate = s.z[:, CONV_DIM:2 * CONV_DIM]
        s.ubuf[32:32 + C, :] = val * _sigmoid(gate)

    def taps(tile):
        def f():
            lanes = slice(tile * LANES, (tile + 1) * LANES)
            rows = 32 + C
            full = s.ubuf[:, lanes]
            shifted = [full] + [pltpu.roll(full, rows - k, 0) for k in range(1, SUBLANES)]
            acc = jnp.zeros((C, LANES), F32) + W.conv_b[:, lanes]
            for j in range(CONV_WIDTH):
                phase, base = (2 + j) % SUBLANES, (2 + j) // SUBLANES * SUBLANES
                acc = acc + shifted[phase][base:base + C, :] * W.conv_w[j:j + 1, lanes]
            v["conv%d" % tile] = acc
            return acc
        return f

    def norm():
        acc = jnp.concatenate([v.pop("conv%d" % t) for t in range(CONV_DIM // LANES)], axis=-1)
        mu = jnp.mean(acc, axis=-1, keepdims=True)
        cen = acc - mu
        var = jnp.mean(cen * cen, axis=-1, keepdims=True)
        hn = cen * lax.rsqrt(var + LN_EPS) * W.ln_g[...] + W.ln_b[...]
        out = hn * _sigmoid(hn)
        s.mix[:, 0:CONV_DIM] = out.astype(BF16)
        s.ubuf[2:32, :] = s.ubuf[C + 2:C + 32, :]
        return _lane_tile_sum(out)

    def pool():
        s.pbuf[16:16 + C, :] = s.z[:, POOL_OFF:POOL_OFF + POOL_DIM]
        pos = pos0 + lax.broadcasted_iota(jnp.int32, (C, POOL_GROUP_DIM), 0)
        outs = []
        for gi, w in enumerate(POOL_WINDOWS):
            lo, hi = gi * POOL_GROUP_DIM, (gi + 1) * POOL_GROUP_DIM
            tok = s.pbuf[16:16 + C, lo:hi]
            tot = tok
            for j in range(1, w):
                tot = tot + s.pbuf[16 - j:16 - j + C, lo:hi]
            cnt = jnp.minimum(w, pos + 1).astype(F32)
            d = tot / cnt - tok
            outs.append(_bdot(d, W.pool_w[gi]))
        out = jnp.concatenate(outs, axis=-1) * W.pool_scale[...]
        s.mix[:, CONV_DIM:CONV_DIM + POOL_DIM] = out.astype(BF16)
        s.pbuf[1:16, :] = s.pbuf[C + 1:C + 16, :]
        return _lane_tile_sum(out)

    return [glu] + [taps(t) for t in range(CONV_DIM // LANES)] + [norm, pool]


def _rwkv_prep_pieces(s, W, ones_bd):
    C = CHUNK
    v = s.v

    def shift(name, off, width):
        def f():
            q = s.z[:, RWKV_OFF + off:RWKV_OFF + off + width]
            rolled = pltpu.roll(q, 1, 0)
            first_row = lax.broadcasted_iota(jnp.int32, (C, width), 0) == 0
            q_prev = jnp.where(first_row, s.qprev[0:1, off:off + width], rolled)
            v[name] = q + (q_prev - q) * W.mu[:, off:off + width]
            s.qprev[0:1, off:off + width] = q[C - 1:C, :]
        return f

    def lora():
        lo = v.pop("lo")
        w_lo, a_lo, g_lo = lo[:, 0:LORA], lo[:, LORA:2 * LORA], lo[:, 2 * LORA:3 * LORA]
        x = W.w0[...] + _bdot(jnp.tanh(w_lo), W.w_up[...])
        v["lw"] = -DECAY_SCALE * _sigmoid(x)
        v["a"] = _sigmoid(W.a0[...] + _bdot(a_lo, W.a_up[...]))
        v["g"] = _bdot(_sigmoid(g_lo), W.g_up[...])

    def keys():
        k, a = v.pop("k"), v["a"]
        kk = k * W.k_k[...]
        kk = kk * lax.rsqrt(jnp.maximum(_head_sums(kk * kk, ones_bd), 1e-24))
        v["kk"] = kk
        v["k2"] = k * (1.0 + (a - 1.0) * W.k_a[...])

    def decay():
        lw = v.pop("lw")
        row = lax.broadcasted_iota(jnp.int32, (C, C), 0)
        col = lax.broadcasted_iota(jnp.int32, (C, C), 1)
        tri = (col <= row).astype(BF16)
        lw_hi = lw.astype(BF16)
        rem = lw - lw_hi.astype(F32)
        lw_mid = rem.astype(BF16)
        lw_lo = (rem - lw_mid.astype(F32)).astype(BF16)
        cum = (jnp.dot(tri, lw_hi, preferred_element_type=F32)
               + jnp.dot(tri, lw_mid, preferred_element_type=F32)
               + jnp.dot(tri, lw_lo, preferred_element_type=F32))
        tot = cum[C - 1:C, :]
        v["e_ex"] = jnp.exp(cum - lw)
        v["e_in"] = jnp.exp(cum)
        v["e_inv"] = jnp.exp(-cum)
        v["e_end"] = jnp.exp(tot - cum)
        v["g_tot"] = jnp.exp(tot)

    def products():
        kk, k2, a = v.pop("kk"), v["k2"], v.pop("a")
        b = kk * a
        e_inv, e_end = v.pop("e_inv"), v.pop("e_end")
        v["A_t"] = -kk * v.pop("e_ex")
        v["R_t"] = v["r"] * v.pop("e_in")
        v["B_t"] = b * e_inv
        v["K_t"] = k2 * e_inv
        v["B_e"] = b * e_end
        v["K_e"] = k2 * e_end

    return [shift("lo", 3 * RWKV_DIM, 3 * LORA), lora, shift("k", RWKV_DIM, RWKV_DIM), keys, decay,
            shift("r", 0, RWKV_DIM), shift("v", 2 * RWKV_DIM, RWKV_DIM), products]


def _wkv_stages(s, pins, never):
    C = CHUNK
    v = s.v
    pairs = range(PAIRS)
    sl = [slice(p * LANES, (p + 1) * LANES) for p in pairs]
    low2 = lax.broadcasted_iota(jnp.int32, (2 * C, LANES), 1) < HEAD_SIZE
    low1 = lax.broadcasted_iota(jnp.int32, (C, LANES), 1) < HEAD_SIZE
    r4 = lax.broadcasted_iota(jnp.int32, (C, 4 * C), 0)
    c4 = lax.broadcasted_iota(jnp.int32, (C, 4 * C), 1) & (C - 1)
    strict = c4 < r4
    incl = c4 <= r4
    rr = lax.broadcasted_iota(jnp.int32, (LANES, LANES), 0)
    cc = lax.broadcasted_iota(jnp.int32, (LANES, LANES), 1)
    same_head = (rr < HEAD_SIZE) == (cc < HEAD_SIZE)
    z_tile = jnp.zeros((C, LANES), BF16)
    z_rows = jnp.zeros((C, 2 * LANES), BF16)
    w = {}

    def masked_rows(p):
        u = w["U"][p]
        return jnp.concatenate([jnp.where(low1, u, 0.0).astype(BF16), w["v_low"][p],
                                jnp.where(low1, 0.0, u).astype(BF16), w["v_high"][p]], axis=0)

    def gram():
        w["S0"] = [s.s[p] for p in pairs]
        w["v_low"] = [jnp.where(low1, v["v"][:, sl[p]], 0.0).astype(BF16) for p in pairs]
        w["v_high"] = [jnp.where(low1, 0.0, v["v"][:, sl[p]]).astype(BF16) for p in pairs]
        A_t, R_t, B_t, K_t = v.pop("A_t"), v.pop("R_t"), v.pop("B_t"), v.pop("K_t")
        GG = []
        for p in pairs:
            AR = jnp.concatenate([A_t[:, sl[p]], R_t[:, sl[p]]], axis=0)
            lhs = jnp.concatenate([jnp.where(low2, AR, 0.0), jnp.where(low2, 0.0, AR)], axis=0)
            rhs = jnp.concatenate([B_t[:, sl[p]], K_t[:, sl[p]], w["S0"][p]], axis=0)
            GG.append(_bdot_nt(lhs, rhs))
        side = lambda g, r0: jnp.concatenate(
            [g[r0:r0 + C, 0:2 * C], g[r0 + 2 * C:r0 + 3 * C, 0:2 * C]], axis=-1)
        w["P"] = [jnp.where(strict, side(g, 0), 0.0).astype(BF16) for g in GG]
        w["M"] = [jnp.where(incl, side(g, C), 0.0).astype(BF16) for g in GG]
        w["U"] = [g[0:C, 2 * C:] + g[2 * C:3 * C, 2 * C:] for g in GG]
        w["AS_bot"] = [g[C:2 * C, 2 * C:] + g[3 * C:4 * C, 2 * C:] for g in GG]

    def take_pins():
        while pins:
            w["U"][PAIRS - 1] = jnp.where(never, pins.pop(), w["U"][PAIRS - 1])

    def level(square):
        def f():
            take_pins()
            Z = [masked_rows(p) for p in pairs]
            P = w["P"]
            w["U"] = [w["U"][p] + jnp.dot(P[p], Z[p], preferred_element_type=F32) for p in pairs]
            if square:
                sq = [jnp.concatenate([
                    jnp.concatenate([P[p][:, 0:2 * C], z_tile], axis=-1), z_rows,
                    jnp.concatenate([z_tile, P[p][:, 2 * C:]], axis=-1), z_rows], axis=0)
                    for p in pairs]
                w["P"] = [jnp.dot(P[p], sq[p], preferred_element_type=F32).astype(BF16)
                          for p in pairs]
        return f

    def finish():
        take_pins()
        B_e, K_e, g_tot = v.pop("B_e"), v.pop("K_e"), v.pop("g_tot")
        ys = []
        for p in pairs:
            ys.append(w["AS_bot"][p]
                      + jnp.dot(w["M"][p], masked_rows(p), preferred_element_type=F32))
            UV = jnp.concatenate([w["U"][p], v["v"][:, sl[p]]], axis=0)
            BKe = jnp.concatenate([B_e[:, sl[p]], K_e[:, sl[p]]], axis=0)
            upd = _bdot(UV.T, BKe)
            s.s[p] = w["S0"][p] * g_tot[:, sl[p]] + jnp.where(same_head, upd, 0.0)
        v["y"] = jnp.concatenate(ys, axis=-1)
        w.clear()

    return [gram] + [level(True) for _ in range(SOLVE_SQUARINGS)] + [level(False), finish]


def _rwkv_post_pieces(s, W, ones_bd):
    v = s.v
    inv_n = 1.0 / HEAD_SIZE

    def center():
        y = v.pop("y")
        v["yc"] = y - _head_sums(y, ones_bd) * inv_n

    def scale():
        yc = v.pop("yc")
        yv = _head_sums(yc * yc, ones_bd) * inv_n
        v["yn"] = yc * lax.rsqrt(yv + GN_EPS) * W.gn_g[...] + W.gn_b[...]

    def bonus():
        v["bonus"] = _head_sums(v.pop("r") * v.pop("k2") * W.r_k[...], ones_bd) * v.pop("v")

    def gate():
        out = (v.pop("yn") + v.pop("bonus")) * v.pop("g")
        s.mix[:, CONV_DIM + POOL_DIM:] = out.astype(BF16)
        return _lane_tile_sum(out)

    return [center, scale, bonus, gate]


def _run_interleaved(stages, fillers, pins):
    n = len(stages)
    for i, stage in enumerate(stages):
        stage()
        for f in fillers[i * len(fillers) // n:(i + 1) * len(fillers) // n]:
            token = f()
            if token is not None:
                pins.append(token)


_MixerWeights = collections.namedtuple(
    "_MixerWeights", "conv_w conv_b ln_g ln_b pool_w pool_scale mu w0 w_up a0 a_up g_up k_k k_a "
                     "r_k gn_g gn_b")


def _mixer_kernel(start_pos, n_seqs, n_chunks, z_ref, hc_ref, hp_ref, hs_ref, s0_ref, ones_ref,
                  *rest):
    n_w = len(_MixerWeights._fields)
    W = _MixerWeights(*rest[:n_w])
    mix_ref, newc_ref, newp_ref, news_ref, news_wkv_ref, ubuf, pbuf, qprev, s_scr = rest[n_w:]
    C = CHUNK
    t = pl.program_id(1)

    @pl.when(t == 0)
    def _():
        zero = jnp.zeros((HEAD_SIZE, HEAD_SIZE), F32)
        for i in range(n_seqs):
            ubuf[i, 0:2, :] = jnp.zeros((2, CONV_DIM), F32)
            ubuf[i, 2:32, :] = hc_ref[i]
            pbuf[i, 0:1, :] = jnp.zeros((1, POOL_DIM), F32)
            pbuf[i, 1:16, :] = hp_ref[i]
            qprev[i] = jnp.zeros(qprev.shape[1:], F32)
            qprev[i, 0:1, :] = hs_ref[i]
            for p in range(PAIRS):
                top = jnp.concatenate([s0_ref[i, 2 * p], zero], axis=-1)
                bot = jnp.concatenate([zero, s0_ref[i, 2 * p + 1]], axis=-1)
                s_scr[i, p] = jnp.concatenate([top, bot], axis=0)

    ones_bd = ones_ref[...]
    ids = [(i, c) for c in range(n_chunks) for i in range(n_seqs)]
    seqs = [_Unit(i, c, z_ref, ubuf, pbuf, qprev, s_scr, mix_ref) for i, c in ids]
    prep = [_rwkv_prep_pieces(s, W, ones_bd) for s in seqs]
    side = [_conv_pool_pieces(s, W, start_pos + (t * n_chunks + c) * C)
            for s, (_, c) in zip(seqs, ids)]
    post = [_rwkv_post_pieces(s, W, ones_bd) for s in seqs]
    pins = []
    never = t < 0
    rounds = [range(i, i + UNITS_PER_ROUND) for i in range(0, len(seqs), UNITS_PER_ROUND)]
    gather = lambda pieces, ids: [f for i in ids for f in pieces[i]]
    for f in gather(prep, rounds[0]):
        f()
    for r, ids in enumerate(rounds):
        stages = [st for group in zip(*[_wkv_stages(seqs[i], pins, never) for i in ids])
                  for st in group]
        fillers = gather(side, ids)
        if r + 1 < len(rounds):
            fillers += gather(prep, rounds[r + 1])
        if r:
            fillers += gather(post, rounds[r - 1])
        _run_interleaved(stages, fillers, pins)
    for f in gather(post, rounds[-1]):
        f()

    @pl.when(t == pl.num_programs(1) - 1)
    def _():
        for i in range(n_seqs):
            newc_ref[i] = ubuf[i, 2:32, :]
            newp_ref[i] = pbuf[i, 1:16, :]
            news_ref[i] = qprev[i, 0:1, :]
            for p in range(PAIRS):
                sp = s_scr[i, p]
                news_wkv_ref[i, 2 * p] = sp[0:HEAD_SIZE, 0:HEAD_SIZE]
                news_wkv_ref[i, 2 * p + 1] = sp[HEAD_SIZE:, HEAD_SIZE:]


def _mixers(z, hist_conv, hist_pool, hist_shift, state_wkv, ones_bd, wts, layer, start_pos):
    B, L, _ = z.shape
    n_chunks = MAX_CHUNKS_PER_STEP if L % (MAX_CHUNKS_PER_STEP * CHUNK) == 0 else 1
    C, G = n_chunks * CHUNK, SEQS_PER_STEP
    assert L % C == 0 and B % G == 0

    def per_layer(shape):
        nd = len(shape)
        return pl.BlockSpec((None,) + shape, lambda b, t: (layer,) + (0,) * nd)

    def per_seq(shape):
        nd = len(shape)
        return pl.BlockSpec((None, G) + shape, lambda b, t: (layer, b) + (0,) * nd)

    def out_seq(shape):
        nd = len(shape)
        return pl.BlockSpec((G,) + shape, lambda b, t: (b,) + (0,) * nd)

    vec = lambda n: per_layer((1, n))
    in_specs = [
        pl.BlockSpec((G, C, IN_PROJ), lambda b, t: (b, t, 0)),
        per_seq((CONV_HIST, CONV_DIM)),
        per_seq((POOL_HIST, POOL_DIM)),
        per_seq((1, RWKV_PROJ)),
        per_seq((RWKV_HEADS, HEAD_SIZE, HEAD_SIZE)),
        pl.BlockSpec((LANES, LANES), lambda b, t: (0, 0)),
        per_layer((CONV_WIDTH, CONV_DIM)), vec(CONV_DIM), vec(CONV_DIM), vec(CONV_DIM),
        per_layer((len(POOL_WINDOWS), POOL_GROUP_DIM, POOL_GROUP_DIM)), vec(POOL_DIM),
        vec(RWKV_PROJ), vec(RWKV_DIM), per_layer((LORA, RWKV_DIM)), vec(RWKV_DIM),
        per_layer((LORA, RWKV_DIM)), per_layer((LORA, RWKV_DIM)),
        vec(RWKV_DIM), vec(RWKV_DIM), vec(RWKV_DIM), vec(RWKV_DIM), vec(RWKV_DIM),
    ]
    out_specs = [
        pl.BlockSpec((G, C, D_MODEL), lambda b, t: (b, t, 0)),
        out_seq((CONV_HIST, CONV_DIM)),
        out_seq((POOL_HIST, POOL_DIM)),
        out_seq((1, RWKV_PROJ)),
        out_seq((RWKV_HEADS, HEAD_SIZE, HEAD_SIZE)),
    ]
    out_shape = [
        jax.ShapeDtypeStruct((B, L, D_MODEL), BF16),
        jax.ShapeDtypeStruct((B, CONV_HIST, CONV_DIM), F32),
        jax.ShapeDtypeStruct((B, POOL_HIST, POOL_DIM), F32),
        jax.ShapeDtypeStruct((B, 1, RWKV_PROJ), F32),
        jax.ShapeDtypeStruct((B, RWKV_HEADS, HEAD_SIZE, HEAD_SIZE), F32),
    ]
    scratch = [
        pltpu.VMEM((G, 32 + CHUNK, CONV_DIM), F32),
        pltpu.VMEM((G, 16 + CHUNK, POOL_DIM), F32),
        pltpu.VMEM((G, SUBLANES, RWKV_PROJ), F32),
        pltpu.VMEM((G, PAIRS, LANES, LANES), F32),
    ]
    return pl.pallas_call(
        functools.partial(_mixer_kernel, start_pos, G, n_chunks),
        grid=(B // G, L // C),
        in_specs=in_specs,
        out_specs=out_specs,
        out_shape=out_shape,
        scratch_shapes=scratch,
        compiler_params=pltpu.CompilerParams(
            dimension_semantics=("arbitrary", "arbitrary"), vmem_limit_bytes=VMEM_LIMIT),
    )(z, hist_conv, hist_pool, hist_shift, state_wkv, ones_bd, *wts)


def _prepare_weights(weights):
    weights = list(weights)
    w_in, w_out, ffn_down = weights[1], weights[19], weights[23]
    weights[1] = jnp.swapaxes(w_in, 1, 2)
    weights[19] = _cast_weight(w_out, 512)
    weights[23] = _cast_weight(ffn_down, 512)
    return tuple(weights)


def _trunk(x, hist_conv, hist_pool, hist_shift, state_wkv, start_pos, weights):
    (norm_mix, w_in, conv_w, conv_b, conv_ln_g, conv_ln_b, pool_w, pool_scale,
     shift_mu, decay_w0, decay_up, iclr_a0, iclr_up, gate_up, k_k, k_a, r_k, gn_g, gn_b,
     w_out, norm_ffn, ffn_gate, ffn_up, ffn_down, norm_final) = weights
    B, L, _ = x.shape
    T = B * L
    as_rows = lambda a: a.reshape(DEPTH, 1, a.shape[-1])
    mixer_wts = (conv_w, as_rows(conv_b), as_rows(conv_ln_g), as_rows(conv_ln_b), pool_w,
                 as_rows(pool_scale), as_rows(shift_mu), as_rows(decay_w0), decay_up,
                 as_rows(iclr_a0), iclr_up, gate_up, as_rows(k_k), as_rows(k_a),
                 as_rows(r_k.reshape(DEPTH, RWKV_DIM)), as_rows(gn_g), as_rows(gn_b))
    norm_mix3 = as_rows(norm_mix)
    idx = jnp.arange(LANES) // HEAD_SIZE
    ones_bd = (idx[:, None] == idx[None, :]).astype(BF16)

    x = x.reshape(T, D_MODEL)
    h = x
    convs, pools, shifts, wkvs = [], [], [], []
    for layer in range(DEPTH):
        last = layer == DEPTH - 1
        z = _in_proj(h, norm_mix3, w_in, layer, fuse_norm=layer == 0)
        mix, c_new, p_new, s_new, S_new = _mixers(
            z.reshape(B, L, IN_PROJ), hist_conv, hist_pool, hist_shift, state_wkv, ones_bd,
            mixer_wts, layer, start_pos)
        x, h = _residual_proj(x, mix.reshape(T, D_MODEL), w_out, layer, norm_ffn[layer][None],
                              final=False)
        act = _ffn_up(h, ffn_gate, ffn_up, layer)
        gain = norm_final[None] if last else norm_mix[layer + 1][None]
        out = _residual_proj(x, act, ffn_down, layer, gain, final=last)
        if not last:
            x, h = out
        convs.append(c_new)
        pools.append(p_new)
        shifts.append(s_new)
        wkvs.append(S_new)
    y = out.reshape(B, L, D_MODEL)
    return y, jnp.stack(convs), jnp.stack(pools), jnp.stack(shifts), jnp.stack(wkvs)


def kernel(x_prompt, x_sample, cache_conv, cache_pool, state_shift, state_wkv, norm_mix, w_in,
           conv_w, conv_b, conv_ln_g, conv_ln_b, pool_w, pool_scale, shift_mu, decay_w0, decay_up,
           iclr_a0, iclr_up, gate_up, k_k, k_a, r_k, gn_g, gn_b, w_out, norm_ffn, ffn_gate,
           ffn_up, ffn_down, norm_final):
    weights = (norm_mix, w_in, conv_w, conv_b, conv_ln_g, conv_ln_b, pool_w, pool_scale,
               shift_mu, decay_w0, decay_up, iclr_a0, iclr_up, gate_up, k_k, k_a, r_k, gn_g, gn_b,
               w_out, norm_ffn, ffn_gate, ffn_up, ffn_down, norm_final)
    weights = _prepare_weights(weights)
    bp = x_prompt.shape[0]
    zc = jnp.zeros((DEPTH, bp, CONV_HIST, CONV_DIM), F32)
    zp = jnp.zeros((DEPTH, bp, POOL_HIST, POOL_DIM), F32)
    zs = jnp.zeros((DEPTH, bp, 1, RWKV_PROJ), F32)
    zw = jnp.zeros((DEPTH, bp, RWKV_HEADS, HEAD_SIZE, HEAD_SIZE), F32)
    y_p, p_conv, p_pool, p_shift, p_wkv = _trunk(x_prompt, zc, zp, zs, zw, 0, weights)
    y_s, s_conv, s_pool, s_shift, s_wkv = _trunk(x_sample, cache_conv, cache_pool, state_shift,
                                                 state_wkv, PAST_LEN, weights)
    return (y_p, y_s, p_conv, p_pool, p_shift, p_wkv, s_conv, s_pool, s_shift, s_wkv)
```

```python
import collections
import functools
import math

import jax
import jax.numpy as jnp
from jax import lax
from jax.experimental import pallas as pl
from jax.experimental.pallas import tpu as pltpu

F32 = jnp.float32
BF16 = jnp.bfloat16

D_MODEL = 2048
DEPTH = 4
PAST_LEN = 4096
CONV_DIM = 512
CONV_WIDTH = 31
CONV_HIST = CONV_WIDTH - 1
POOL_DIM = 512
POOL_WINDOWS = (2, 4, 8, 16)
POOL_GROUP_DIM = 128
POOL_HIST = max(POOL_WINDOWS) - 1
RWKV_DIM = 1024
HEAD_SIZE = 64
RWKV_HEADS = RWKV_DIM // HEAD_SIZE
LORA = 64
RWKV_PROJ = 3 * RWKV_DIM + 3 * LORA
IN_PROJ = 2 * CONV_DIM + POOL_DIM + RWKV_PROJ
POOL_OFF = 2 * CONV_DIM
RWKV_OFF = POOL_OFF + POOL_DIM
D_FF = 5632
RMS_EPS = 1e-6
DECAY_SCALE = math.exp(-0.5)
LN_EPS = 1e-5
GN_EPS = 64e-5

LANES = 128
SUBLANES = 8
PAIRS = RWKV_DIM // LANES
SEQS_PER_STEP = 4
MAX_CHUNKS_PER_STEP = 2
UNITS_PER_ROUND = 2
CHUNK = 64
SOLVE_SQUARINGS = 5
VMEM_LIMIT = 56 * 1024 * 1024


def _tiles(n_tokens):
    tm = min(n_tokens, 1024)
    assert n_tokens % tm == 0
    return dict(tm=tm, tn=1024, tn_ff=512, tk_res=512)


def _cast_kernel(w_ref, o_ref):
    o_ref[...] = w_ref[...].astype(BF16)


def _cast_weight(w, rows):
    depth, K, N = w.shape
    return pl.pallas_call(
        _cast_kernel,
        grid=(depth, K // rows),
        in_specs=[pl.BlockSpec((None, rows, N), lambda l, k: (l, k, 0))],
        out_specs=pl.BlockSpec((None, rows, N), lambda l, k: (l, k, 0)),
        out_shape=jax.ShapeDtypeStruct(w.shape, BF16),
        compiler_params=pltpu.CompilerParams(dimension_semantics=("arbitrary", "arbitrary")),
    )(w)


def _bdot(a, b):
    return jnp.dot(a.astype(BF16), b.astype(BF16), preferred_element_type=F32)


def _bdot_nt(a, b):
    return lax.dot_general(a.astype(BF16), b.astype(BF16), (((1,), (1,)), ((), ())),
                           preferred_element_type=F32)


def _sigmoid(x):
    return 1.0 / (1.0 + jnp.exp(-x))


def _rms_norm(x, gain):
    ms = jnp.mean(x * x, axis=-1, keepdims=True)
    return x * lax.rsqrt(ms + RMS_EPS) * gain


def _in_proj_kernel(fuse_norm, x_ref, g_ref, w_ref, o_ref, *scratch):
    if fuse_norm:
        h_ref, = scratch

        @pl.when(pl.program_id(1) == 0)
        def _():
            h_ref[...] = _rms_norm(x_ref[...], g_ref[...]).astype(BF16)

        h = h_ref[...]
    else:
        h = x_ref[...]
    o_ref[...] = lax.dot_general(h, w_ref[...].astype(BF16), (((1,), (1,)), ((), ())),
                                 preferred_element_type=F32)


def _in_proj(x, g, w, layer, fuse_norm):
    T = x.shape[0]
    t = _tiles(T)
    tm, tn = t["tm"], t["tn"]
    return pl.pallas_call(
        functools.partial(_in_proj_kernel, fuse_norm),
        grid=(T // tm, pl.cdiv(IN_PROJ, tn)),
        in_specs=[
            pl.BlockSpec((tm, D_MODEL), lambda m, n: (m, 0)),
            pl.BlockSpec((None, 1, D_MODEL), lambda m, n: (layer, 0, 0)),
            pl.BlockSpec((None, tn, D_MODEL), lambda m, n: (layer, n, 0)),
        ],
        out_specs=pl.BlockSpec((tm, tn), lambda m, n: (m, n)),
        out_shape=jax.ShapeDtypeStruct((T, IN_PROJ), F32),
        scratch_shapes=[pltpu.VMEM((tm, D_MODEL), BF16)] if fuse_norm else [],
        compiler_params=pltpu.CompilerParams(
            dimension_semantics=("arbitrary", "arbitrary"), vmem_limit_bytes=VMEM_LIMIT),
    )(x, g, w)


def _ffn_up_kernel(h_ref, wg_ref, wu_ref, o_ref, wg_bf, wu_bf):
    @pl.when(pl.program_id(1) == 0)
    def _():
        wg_bf[...] = wg_ref[...].astype(BF16)
        wu_bf[...] = wu_ref[...].astype(BF16)

    h = h_ref[...]
    gate = jnp.dot(h, wg_bf[...], preferred_element_type=F32)
    up = jnp.dot(h, wu_bf[...], preferred_element_type=F32)
    o_ref[...] = (gate * _sigmoid(gate) * up).astype(BF16)


def _ffn_up(h, wg, wu, layer):
    T = h.shape[0]
    t = _tiles(T)
    tm, tn = t["tm"], t["tn_ff"]
    return pl.pallas_call(
        _ffn_up_kernel,
        grid=(D_FF // tn, T // tm),
        in_specs=[
            pl.BlockSpec((tm, D_MODEL), lambda n, m: (m, 0)),
            pl.BlockSpec((None, D_MODEL, tn), lambda n, m: (layer, 0, n)),
            pl.BlockSpec((None, D_MODEL, tn), lambda n, m: (layer, 0, n)),
        ],
        out_specs=pl.BlockSpec((tm, tn), lambda n, m: (m, n)),
        out_shape=jax.ShapeDtypeStruct((T, D_FF), BF16),
        scratch_shapes=[pltpu.VMEM((D_MODEL, tn), BF16), pltpu.VMEM((D_MODEL, tn), BF16)],
        compiler_params=pltpu.CompilerParams(
            dimension_semantics=("arbitrary", "arbitrary"), vmem_limit_bytes=VMEM_LIMIT),
    )(h, wg, wu)


def _residual_proj_kernel(final, x_ref, a_ref, w_ref, g_ref, o_ref, *h_ref):
    k = pl.program_id(1)

    @pl.when(k == 0)
    def _():
        o_ref[...] = x_ref[...]

    o_ref[...] += jnp.dot(a_ref[...], w_ref[...], preferred_element_type=F32)

    @pl.when(k == pl.num_programs(1) - 1)
    def _():
        normed = _rms_norm(o_ref[...], g_ref[...])
        if final:
            o_ref[...] = normed
        else:
            h_ref[0][...] = normed.astype(BF16)


def _residual_proj(x, a, w, layer, gain, final):
    T, K = a.shape
    t = _tiles(T)
    tm, tk = t["tm"], t["tk_res"]
    row_spec = pl.BlockSpec((tm, D_MODEL), lambda m, k: (m, 0))
    out_shape = [jax.ShapeDtypeStruct((T, D_MODEL), F32)]
    if not final:
        out_shape.append(jax.ShapeDtypeStruct((T, D_MODEL), BF16))
    out = pl.pallas_call(
        functools.partial(_residual_proj_kernel, final),
        grid=(T // tm, K // tk),
        in_specs=[
            row_spec,
            pl.BlockSpec((tm, tk), lambda m, k: (m, k)),
            pl.BlockSpec((None, tk, D_MODEL), lambda m, k: (layer, k, 0)),
            pl.BlockSpec((1, D_MODEL), lambda m, k: (0, 0)),
        ],
        out_specs=[row_spec] * len(out_shape),
        out_shape=out_shape,
        compiler_params=pltpu.CompilerParams(
            dimension_semantics=("arbitrary", "arbitrary"), vmem_limit_bytes=VMEM_LIMIT),
    )(x, a, w, gain)
    return out[0] if final else out


class _Unit:
    def __init__(self, i, c, z_ref, ubuf, pbuf, qprev, s_scr, mix_ref):
        rows = pl.ds(c * CHUNK, CHUNK)
        self.z = z_ref.at[i, rows]
        self.ubuf = ubuf.at[i]
        self.pbuf = pbuf.at[i]
        self.qprev = qprev.at[i]
        self.s = s_scr.at[i]
        self.mix = mix_ref.at[i, rows]
        self.v = {}


def _head_sums(x, ones_bd):
    stacked = jnp.concatenate([x[:, p * LANES:(p + 1) * LANES] for p in range(PAIRS)], axis=0)
    s = _bdot(stacked, ones_bd)
    C = x.shape[0]
    return jnp.concatenate([s[p * C:(p + 1) * C, :] for p in range(PAIRS)], axis=-1)


def _lane_tile_sum(x):
    tiles = [x[:, i * LANES:(i + 1) * LANES] for i in range(x.shape[-1] // LANES)]
    while len(tiles) > 1:
        tiles = [a + b for a, b in zip(tiles[0::2], tiles[1::2])] + tiles[len(tiles) & ~1:]
    return tiles[0]


def _conv_pool_pieces(s, W, pos0):
    C = CHUNK
    v = s.v

    def glu():
        val = s.z[:, 0:CONV_DIM]
        gate = s.z[:, CONV_DIM:2 * CONV_DIM]
        s.ubuf[32:32 + C, :] = val * _sigmoid(gate)

    def taps(tile):
        def f():
            lanes = slice(tile * LANES, (tile + 1) * LANES)
            rows = 32 + C
            full = s.ubuf[:, lanes]
            shifted = [full] + [pltpu.roll(full, rows - k, 0) for k in range(1, SUBLANES)]
            acc = jnp.zeros((C, LANES), F32) + W.conv_b[:, lanes]
            for j in range(CONV_WIDTH):
                phase, base = (2 + j) % SUBLANES, (2 + j) // SUBLANES * SUBLANES
                acc = acc + shifted[phase][base:base + C, :] * W.conv_w[j:j + 1, lanes]
            v["conv%d" % tile] = acc
            return acc
        return f

    def norm():
        acc = jnp.concatenate([v.pop("conv%d" % t) for t in range(CONV_DIM // LANES)], axis=-1)
        mu = jnp.mean(acc, axis=-1, keepdims=True)
        cen = acc - mu
        var = jnp.mean(cen * cen, axis=-1, keepdims=True)
        hn = cen * lax.rsqrt(var + LN_EPS) * W.ln_g[...] + W.ln_b[...]
        out = hn * _sigmoid(hn)
        s.mix[:, 0:CONV_DIM] = out.astype(BF16)
        s.ubuf[2:32, :] = s.ubuf[C + 2:C + 32, :]
        return _lane_tile_sum(out)

    def pool():
        s.pbuf[16:16 + C, :] = s.z[:, POOL_OFF:POOL_OFF + POOL_DIM]
        pos = pos0 + lax.broadcasted_iota(jnp.int32, (C, POOL_GROUP_DIM), 0)
        outs = []
        for gi, w in enumerate(POOL_WINDOWS):
            lo, hi = gi * POOL_GROUP_DIM, (gi + 1) * POOL_GROUP_DIM
            tok = s.pbuf[16:16 + C, lo:hi]
            tot = tok
            for j in range(1, w):
                tot = tot + s.pbuf[16 - j:16 - j + C, lo:hi]
            cnt = jnp.minimum(w, pos + 1).astype(F32)
            d = tot / cnt - tok
            outs.append(_bdot(d, W.pool_w[gi]))
        out = jnp.concatenate(outs, axis=-1) * W.pool_scale[...]
        s.mix[:, CONV_DIM:CONV_DIM + POOL_DIM] = out.astype(BF16)
        s.pbuf[1:16, :] = s.pbuf[C + 1:C + 16, :]
        return _lane_tile_sum(out)

    return [glu] + [taps(t) for t in range(CONV_DIM // LANES)] + [norm, pool]


def _rwkv_prep_pieces(s, W, ones_bd):
    C = CHUNK
    v = s.v

    def shift(name, off, width):
        def f():
            q = s.z[:, RWKV_OFF + off:RWKV_OFF + off + width]
            rolled = pltpu.roll(q, 1, 0)
            first_row = lax.broadcasted_iota(jnp.int32, (C, width), 0) == 0
            q_prev = jnp.where(first_row, s.qprev[0:1, off:off + width], rolled)
            v[name] = q + (q_prev - q) * W.mu[:, off:off + width]
            s.qprev[0:1, off:off + width] = q[C - 1:C, :]
        return f

    def lora():
        lo = v.pop("lo")
        w_lo, a_lo, g_lo = lo[:, 0:LORA], lo[:, LORA:2 * LORA], lo[:, 2 * LORA:3 * LORA]
        x = W.w0[...] + _bdot(jnp.tanh(w_lo), W.w_up[...])
        v["lw"] = -DECAY_SCALE * _sigmoid(x)
        v["a"] = _sigmoid(W.a0[...] + _bdot(a_lo, W.a_up[...]))
        v["g"] = _bdot(_sigmoid(g_lo), W.g_up[...])

    def keys():
        k, a = v.pop("k"), v["a"]
        kk = k * W.k_k[...]
        kk = kk * lax.rsqrt(jnp.maximum(_head_sums(kk * kk, ones_bd), 1e-24))
        v["kk"] = kk
        v["k2"] = k * (1.0 + (a - 1.0) * W.k_a[...])

    def decay():
        lw = v.pop("lw")
        row = lax.broadcasted_iota(jnp.int32, (C, C), 0)
        col = lax.broadcasted_iota(jnp.int32, (C, C), 1)
        tri = (col <= row).astype(BF16)
        lw_hi = lw.astype(BF16)
        rem = lw - lw_hi.astype(F32)
        lw_mid = rem.astype(BF16)
        lw_lo = (rem - lw_mid.astype(F32)).astype(BF16)
        cum = (jnp.dot(tri, lw_hi, preferred_element_type=F32)
               + jnp.dot(tri, lw_mid, preferred_element_type=F32)
               + jnp.dot(tri, lw_lo, preferred_element_type=F32))
        tot = cum[C - 1:C, :]
        v["e_ex"] = jnp.exp(cum - lw)
        v["e_in"] = jnp.exp(cum)
        v["e_inv"] = jnp.exp(-cum)
        v["e_end"] = jnp.exp(tot - cum)
        v["g_tot"] = jnp.exp(tot)

    def products():
        kk, k2, a = v.pop("kk"), v["k2"], v.pop("a")
        b = kk * a
        e_inv, e_end = v.pop("e_inv"), v.pop("e_end")
        v["A_t"] = -kk * v.pop("e_ex")
        v["R_t"] = v["r"] * v.pop("e_in")
        v["B_t"] = b * e_inv
        v["K_t"] = k2 * e_inv
        v["B_e"] = b * e_end
        v["K_e"] = k2 * e_end

    return [shift("lo", 3 * RWKV_DIM, 3 * LORA), lora, shift("k", RWKV_DIM, RWKV_DIM), keys, decay,
            shift("r", 0, RWKV_DIM), shift("v", 2 * RWKV_DIM, RWKV_DIM), products]


def _wkv_stages(s, pins, never):
    C = CHUNK
    v = s.v
    pairs = range(PAIRS)
    sl = [slice(p * LANES, (p + 1) * LANES) for p in pairs]
    low2 = lax.broadcasted_iota(jnp.int32, (2 * C, LANES), 1) < HEAD_SIZE
    low1 = lax.broadcasted_iota(jnp.int32, (C, LANES), 1) < HEAD_SIZE
    r4 = lax.broadcasted_iota(jnp.int32, (C, 4 * C), 0)
    c4 = lax.broadcasted_iota(jnp.int32, (C, 4 * C), 1) & (C - 1)
    strict = c4 < r4
    incl = c4 <= r4
    rr = lax.broadcasted_iota(jnp.int32, (LANES, LANES), 0)
    cc = lax.broadcasted_iota(jnp.int32, (LANES, LANES), 1)
    same_head = (rr < HEAD_SIZE) == (cc < HEAD_SIZE)
    z_tile = jnp.zeros((C, LANES), BF16)
    z_rows = jnp.zeros((C, 2 * LANES), BF16)
    w = {}

    def masked_rows(p):
        u = w["U"][p]
        return jnp.concatenate([jnp.where(low1, u, 0.0).astype(BF16), w["v_low"][p],
                                jnp.where(low1, 0.0, u).astype(BF16), w["v_high"][p]], axis=0)

    def gram():
        w["S0"] = [s.s[p] for p in pairs]
        w["v_low"] = [jnp.where(low1, v["v"][:, sl[p]], 0.0).astype(BF16) for p in pairs]
        w["v_high"] = [jnp.where(low1, 0.0, v["v"][:, sl[p]]).astype(BF16) for p in pairs]
        A_t, R_t, B_t, K_t = v.pop("A_t"), v.pop("R_t"), v.pop("B_t"), v.pop("K_t")
        GG = []
        for p in pairs:
            AR = jnp.concatenate([A_t[:, sl[p]], R_t[:, sl[p]]], axis=0)
            lhs = jnp.concatenate([jnp.where(low2, AR, 0.0), jnp.where(low2, 0.0, AR)], axis=0)
            rhs = jnp.concatenate([B_t[:, sl[p]], K_t[:, sl[p]], w["S0"][p]], axis=0)
            GG.append(_bdot_nt(lhs, rhs))
        side = lambda g, r0: jnp.concatenate(
            [g[r0:r0 + C, 0:2 * C], g[r0 + 2 * C:r0 + 3 * C, 0:2 * C]], axis=-1)
        w["P"] = [jnp.where(strict, side(g, 0), 0.0).astype(BF16) for g in GG]
        w["M"] = [jnp.where(incl, side(g, C), 0.0).astype(BF16) for g in GG]
        w["U"] = [g[0:C, 2 * C:] + g[2 * C:3 * C, 2 * C:] for g in GG]
        w["AS_bot"] = [g[C:2 * C, 2 * C:] + g[3 * C:4 * C, 2 * C:] for g in GG]

    def take_pins():
        while pins:
            w["U"][PAIRS - 1] = jnp.where(never, pins.pop(), w["U"][PAIRS - 1])

    def level(square):
        def f():
            take_pins()
            Z = [masked_rows(p) for p in pairs]
            P = w["P"]
            w["U"] = [w["U"][p] + jnp.dot(P[p], Z[p], preferred_element_type=F32) for p in pairs]
            if square:
                sq = [jnp.concatenate([
                    jnp.concatenate([P[p][:, 0:2 * C], z_tile], axis=-1), z_rows,
                    jnp.concatenate([z_tile, P[p][:, 2 * C:]], axis=-1), z_rows], axis=0)
                    for p in pairs]
                w["P"] = [jnp.dot(P[p], sq[p], preferred_element_type=F32).astype(BF16)
                          for p in pairs]
        return f

    def finish():
        take_pins()
        B_e, K_e, g_tot = v.pop("B_e"), v.pop("K_e"), v.pop("g_tot")
        ys = []
        for p in pairs:
            ys.append(w["AS_bot"][p]
                      + jnp.dot(w["M"][p], masked_rows(p), preferred_element_type=F32))
            UV = jnp.concatenate([w["U"][p], v["v"][:, sl[p]]], axis=0)
            BKe = jnp.concatenate([B_e[:, sl[p]], K_e[:, sl[p]]], axis=0)
            upd = _bdot(UV.T, BKe)
            s.s[p] = w["S0"][p] * g_tot[:, sl[p]] + jnp.where(same_head, upd, 0.0)
        v["y"] = jnp.concatenate(ys, axis=-1)
        w.clear()

    return [gram] + [level(True) for _ in range(SOLVE_SQUARINGS)] + [level(False), finish]


def _rwkv_post_pieces(s, W, ones_bd):
    v = s.v
    inv_n = 1.0 / HEAD_SIZE

    def center():
        y = v.pop("y")
        v["yc"] = y - _head_sums(y, ones_bd) * inv_n

    def scale():
        yc = v.pop("yc")
        yv = _head_sums(yc * yc, ones_bd) * inv_n
        v["yn"] = yc * lax.rsqrt(yv + GN_EPS) * W.gn_g[...] + W.gn_b[...]

    def bonus():
        v["bonus"] = _head_sums(v.pop("r") * v.pop("k2") * W.r_k[...], ones_bd) * v.pop("v")

    def gate():
        out = (v.pop("yn") + v.pop("bonus")) * v.pop("g")
        s.mix[:, CONV_DIM + POOL_DIM:] = out.astype(BF16)
        return _lane_tile_sum(out)

    return [center, scale, bonus, gate]


def _run_interleaved(stages, fillers, pins):
    n = len(stages)
    for i, stage in enumerate(stages):
        stage()
        for f in fillers[i * len(fillers) // n:(i + 1) * len(fillers) // n]:
            token = f()
            if token is not None:
                pins.append(token)


_MixerWeights = collections.namedtuple(
    "_MixerWeights", "conv_w conv_b ln_g ln_b pool_w pool_scale mu w0 w_up a0 a_up g_up k_k k_a "
                     "r_k gn_g gn_b")


def _mixer_kernel(start_pos, n_seqs, n_chunks, z_ref, hc_ref, hp_ref, hs_ref, s0_ref, ones_ref,
                  *rest):
    n_w = len(_MixerWeights._fields)
    W = _MixerWeights(*rest[:n_w])
    mix_ref, newc_ref, newp_ref, news_ref, news_wkv_ref, ubuf, pbuf, qprev, s_scr = rest[n_w:]
    C = CHUNK
    t = pl.program_id(1)

    @pl.when(t == 0)
    def _():
        zero = jnp.zeros((HEAD_SIZE, HEAD_SIZE), F32)
        for i in range(n_seqs):
            ubuf[i, 0:2, :] = jnp.zeros((2, CONV_DIM), F32)
            ubuf[i, 2:32, :] = hc_ref[i]
            pbuf[i, 0:1, :] = jnp.zeros((1, POOL_DIM), F32)
            pbuf[i, 1:16, :] = hp_ref[i]
            qprev[i] = jnp.zeros(qprev.shape[1:], F32)
            qprev[i, 0:1, :] = hs_ref[i]
            for p in range(PAIRS):
                top = jnp.concatenate([s0_ref[i, 2 * p], zero], axis=-1)
                bot = jnp.concatenate([zero, s0_ref[i, 2 * p + 1]], axis=-1)
                s_scr[i, p] = jnp.concatenate([top, bot], axis=0)

    ones_bd = ones_ref[...]
    ids = [(i, c) for c in range(n_chunks) for i in range(n_seqs)]
    seqs = [_Unit(i, c, z_ref, ubuf, pbuf, qprev, s_scr, mix_ref) for i, c in ids]
    prep = [_rwkv_prep_pieces(s, W, ones_bd) for s in seqs]
    side = [_conv_pool_pieces(s, W, start_pos + (t * n_chunks + c) * C)
            for s, (_, c) in zip(seqs, ids)]
    post = [_rwkv_post_pieces(s, W, ones_bd) for s in seqs]
    pins = []
    never = t < 0
    rounds = [range(i, i + UNITS_PER_ROUND) for i in range(0, len(seqs), UNITS_PER_ROUND)]
    gather = lambda pieces, ids: [f for i in ids for f in pieces[i]]
    for f in gather(prep, rounds[0]):
        f()
    for r, ids in enumerate(rounds):
        stages = [st for group in zip(*[_wkv_stages(seqs[i], pins, never) for i in ids])
                  for st in group]
        fillers = gather(side, ids)
        if r + 1 < len(rounds):
            fillers += gather(prep, rounds[r + 1])
        if r:
            fillers += gather(post, rounds[r - 1])
        _run_interleaved(stages, fillers, pins)
    for f in gather(post, rounds[-1]):
        f()

    @pl.when(t == pl.num_programs(1) - 1)
    def _():
        for i in range(n_seqs):
            newc_ref[i] = ubuf[i, 2:32, :]
            newp_ref[i] = pbuf[i, 1:16, :]
            news_ref[i] = qprev[i, 0:1, :]
            for p in range(PAIRS):
                sp = s_scr[i, p]
                news_wkv_ref[i, 2 * p] = sp[0:HEAD_SIZE, 0:HEAD_SIZE]
                news_wkv_ref[i, 2 * p + 1] = sp[HEAD_SIZE:, HEAD_SIZE:]


def _mixers(z, hist_conv, hist_pool, hist_shift, state_wkv, ones_bd, wts, layer, start_pos):
    B, L, _ = z.shape
    n_chunks = MAX_CHUNKS_PER_STEP if L % (MAX_CHUNKS_PER_STEP * CHUNK) == 0 else 1
    C, G = n_chunks * CHUNK, SEQS_PER_STEP
    assert L % C == 0 and B % G == 0

    def per_layer(shape):
        nd = len(shape)
        return pl.BlockSpec((None,) + shape, lambda b, t: (layer,) + (0,) * nd)

    def per_seq(shape):
        nd = len(shape)
        return pl.BlockSpec((None, G) + shape, lambda b, t: (layer, b) + (0,) * nd)

    def out_seq(shape):
        nd = len(shape)
        return pl.BlockSpec((G,) + shape, lambda b, t: (b,) + (0,) * nd)

    vec = lambda n: per_layer((1, n))
    in_specs = [
        pl.BlockSpec((G, C, IN_PROJ), lambda b, t: (b, t, 0)),
        per_seq((CONV_HIST, CONV_DIM)),
        per_seq((POOL_HIST, POOL_DIM)),
        per_seq((1, RWKV_PROJ)),
        per_seq((RWKV_HEADS, HEAD_SIZE, HEAD_SIZE)),
        pl.BlockSpec((LANES, LANES), lambda b, t: (0, 0)),
        per_layer((CONV_WIDTH, CONV_DIM)), vec(CONV_DIM), vec(CONV_DIM), vec(CONV_DIM),
        per_layer((len(POOL_WINDOWS), POOL_GROUP_DIM, POOL_GROUP_DIM)), vec(POOL_DIM),
        vec(RWKV_PROJ), vec(RWKV_DIM), per_layer((LORA, RWKV_DIM)), vec(RWKV_DIM),
        per_layer((LORA, RWKV_DIM)), per_layer((LORA, RWKV_DIM)),
        vec(RWKV_DIM), vec(RWKV_DIM), vec(RWKV_DIM), vec(RWKV_DIM), vec(RWKV_DIM),
    ]
    out_specs = [
        pl.BlockSpec((G, C, D_MODEL), lambda b, t: (b, t, 0)),
        out_seq((CONV_HIST, CONV_DIM)),
        out_seq((POOL_HIST, POOL_DIM)),
        out_seq((1, RWKV_PROJ)),
        out_seq((RWKV_HEADS, HEAD_SIZE, HEAD_SIZE)),
    ]
    out_shape = [
        jax.ShapeDtypeStruct((B, L, D_MODEL), BF16),
        jax.ShapeDtypeStruct((B, CONV_HIST, CONV_DIM), F32),
        jax.ShapeDtypeStruct((B, POOL_HIST, POOL_DIM), F32),
        jax.ShapeDtypeStruct((B, 1, RWKV_PROJ), F32),
        jax.ShapeDtypeStruct((B, RWKV_HEADS, HEAD_SIZE, HEAD_SIZE), F32),
    ]
    scratch = [
        pltpu.VMEM((G, 32 + CHUNK, CONV_DIM), F32),
        pltpu.VMEM((G, 16 + CHUNK, POOL_DIM), F32),
        pltpu.VMEM((G, SUBLANES, RWKV_PROJ), F32),
        pltpu.VMEM((G, PAIRS, LANES, LANES), F32),
    ]
    return pl.pallas_call(
        functools.partial(_mixer_kernel, start_pos, G, n_chunks),
        grid=(B // G, L // C),
        in_specs=in_specs,
        out_specs=out_specs,
        out_shape=out_shape,
        scratch_shapes=scratch,
        compiler_params=pltpu.CompilerParams(
            dimension_semantics=("arbitrary", "arbitrary"), vmem_limit_bytes=VMEM_LIMIT),
    )(z, hist_conv, hist_pool, hist_shift, state_wkv, ones_bd, *wts)


def _prepare_weights(weights):
    weights = list(weights)
    w_in, w_out, ffn_down = weights[1], weights[19], weights[23]
    weights[1] = jnp.swapaxes(w_in, 1, 2)
    weights[19] = _cast_weight(w_out, 512)
    weights[23] = _cast_weight(ffn_down, 512)
    return tuple(weights)


def _trunk(x, hist_conv, hist_pool, hist_shift, state_wkv, start_pos, weights):
    (norm_mix, w_in, conv_w, conv_b, conv_ln_g, conv_ln_b, pool_w, pool_scale,
     shift_mu, decay_w0, decay_up, iclr_a0, iclr_up, gate_up, k_k, k_a, r_k, gn_g, gn_b,
     w_out, norm_ffn, ffn_gate, ffn_up, ffn_down, norm_final) = weights
    B, L, _ = x.shape
    T = B * L
    as_rows = lambda a: a.reshape(DEPTH, 1, a.shape[-1])
    mixer_wts = (conv_w, as_rows(conv_b), as_rows(conv_ln_g), as_rows(conv_ln_b), pool_w,
                 as_rows(pool_scale), as_rows(shift_mu), as_rows(decay_w0), decay_up,
                 as_rows(iclr_a0), iclr_up, gate_up, as_rows(k_k), as_rows(k_a),
                 as_rows(r_k.reshape(DEPTH, RWKV_DIM)), as_rows(gn_g), as_rows(gn_b))
    norm_mix3 = as_rows(norm_mix)
    idx = jnp.arange(LANES) // HEAD_SIZE
    ones_bd = (idx[:, None] == idx[None, :]).astype(BF16)

    x = x.reshape(T, D_MODEL)
    h = x
    convs, pools, shifts, wkvs = [], [], [], []
    for layer in range(DEPTH):
        last = layer == DEPTH - 1
        z = _in_proj(h, norm_mix3, w_in, layer, fuse_norm=layer == 0)
        mix, c_new, p_new, s_new, S_new = _mixers(
            z.reshape(B, L, IN_PROJ), hist_conv, hist_pool, hist_shift, state_wkv, ones_bd,
            mixer_wts, layer, start_pos)
        x, h = _residual_proj(x, mix.reshape(T, D_MODEL), w_out, layer, norm_ffn[layer][None],
                              final=False)
        act = _ffn_up(h, ffn_gate, ffn_up, layer)
        gain = norm_final[None] if last else norm_mix[layer + 1][None]
        out = _residual_proj(x, act, ffn_down, layer, gain, final=last)
        if not last:
            x, h = out
        convs.append(c_new)
        pools.append(p_new)
        shifts.append(s_new)
        wkvs.append(S_new)
    y = out.reshape(B, L, D_MODEL)
    return y, jnp.stack(convs), jnp.stack(pools), jnp.stack(shifts), jnp.stack(wkvs)


def kernel(x_prompt, x_sample, cache_conv, cache_pool, state_shift, state_wkv, norm_mix, w_in,
           conv_w, conv_b, conv_ln_g, conv_ln_b, pool_w, pool_scale, shift_mu, decay_w0, decay_up,
           iclr_a0, iclr_up, gate_up, k_k, k_a, r_k, gn_g, gn_b, w_out, norm_ffn, ffn_gate,
           ffn_up, ffn_down, norm_final):
    weights = (norm_mix, w_in, conv_w, conv_b, conv_ln_g, conv_ln_b, pool_w, pool_scale,
               shift_mu, decay_w0, decay_up, iclr_a0, iclr_up, gate_up, k_k, k_a, r_k, gn_g, gn_b,
               w_out, norm_ffn, ffn_gate, ffn_up, ffn_down, norm_final)
    weights = _prepare_weights(weights)
    bp = x_prompt.shape[0]
    zc = jnp.zeros((DEPTH, bp, CONV_HIST, CONV_DIM), F32)
    zp = jnp.zeros((DEPTH, bp, POOL_HIST, POOL_DIM), F32)
    zs = jnp.zeros((DEPTH, bp, 1, RWKV_PROJ), F32)
    zw = jnp.zeros((DEPTH, bp, RWKV_HEADS, HEAD_SIZE, HEAD_SIZE), F32)
    y_p, p_conv, p_pool, p_shift, p_wkv = _trunk(x_prompt, zc, zp, zs, zw, 0, weights)
    y_s, s_conv, s_pool, s_shift, s_wkv = _trunk(x_sample, cache_conv, cache_pool, state_shift,
                                                 state_wkv, PAST_LEN, weights)
    return (y_p, y_s, p_conv, p_pool, p_shift, p_wkv, s_conv, s_pool, s_shift, s_wkv)
```
